```python
import math
import jax
import jax.numpy as jnp
from jax import lax
import numpy as np

D_MODEL = 2048
BATCH = 2
SEQ = 4096
DEPTH = 4
DEC_BATCH = 32
DEC_SEQ = 8
PAST_LEN = 16384
PAGE_SIZE = 128

N_A = (DEPTH + 1) // 2
N_B = DEPTH // 2
A_HEAD = 64
A_HEADS = D_MODEL // A_HEAD
LORA_DECAY = 96
LORA_AAA = 96
LORA_MV = 64
LORA_GATE = 256
LNX_EPS = 64e-5
B_HEAD = 64
B_HEADS = D_MODEL // B_HEAD
B_KV_HEADS = B_HEADS // 8
B_GROUP = B_HEADS // B_KV_HEADS
WINDOW = 128
SM_SCALE = B_HEAD ** -0.5
N_BUCKETS = 32
MAX_DISTANCE = 128
N_EXPERTS = 32
TOP_K = 4
D_EXPERT = D_MODEL
SWIGLU_ALPHA = 1.702
SWIGLU_LIMIT = 7.0
EXPERT_BLOCK = 128
RMS_EPS = 1e-5

kernel_name = 'rwkv7_swa_sink_moe_adaln_step'


def rms_norm(x, g):
    xf = x.astype(jnp.float32)
    y = xf * lax.rsqrt(jnp.mean(xf * xf, axis=-1, keepdims=True) + RMS_EPS)
    return (y * g.astype(jnp.float32)).astype(x.dtype)


def t5_bucket(dist):
    max_exact = N_BUCKETS // 2
    n = jnp.maximum(dist, 0)
    nf = jnp.maximum(n, max_exact).astype(jnp.float32)
    large = max_exact + (jnp.log(nf / max_exact) / math.log(MAX_DISTANCE / max_exact)
                         * (N_BUCKETS - max_exact)).astype(jnp.int32)
    return jnp.where(n < max_exact, n, jnp.minimum(large, N_BUCKETS - 1))


def rel_bias(dist, table):
    b = jnp.transpose(table[t5_bucket(dist)], (2, 0, 1)).astype(jnp.float32)
    return b.reshape(B_KV_HEADS, B_GROUP, dist.shape[0], dist.shape[1])


def wkv_scan(S0, r, decay, k, v, a, b):
    def step(S, inp):
        r_t, w_t, k_t, v_t, a_t, b_t = inp
        Sa = jnp.einsum('nhvk,nhk->nhv', S, a_t)
        S = S * w_t[:, :, None, :] + Sa[..., None] * b_t[:, :, None, :] + v_t[..., None] * k_t[:, :, None, :]
        return S, jnp.einsum('nhvk,nhk->nhv', S, r_t)
    xs = tuple(jnp.moveaxis(t, 1, 0) for t in (r, decay, k, v, a, b))
    S, o = lax.scan(step, S0, xs)
    return S, jnp.moveaxis(o, 0, 1)


def rwkv7_mix(h, shift0, S0, v_first, vres, mix, w_rkv, w0, w1, w2, a0, a1, a2,
              g1, g2, k_k, k_a, r_k, lnx_w, lnx_b, w_o):
    f32 = jnp.float32
    Nb, T, D = h.shape
    h_prev = jnp.concatenate([shift0[:, None, :].astype(h.dtype), h[:, :-1]], axis=1)
    xm = h[:, :, None, :] + (h_prev - h)[:, :, None, :] * mix
    rkv = jnp.einsum('ntjd,jde->ntje', xm[:, :, :3], w_rkv)
    r, k, v = rkv[:, :, 0], rkv[:, :, 1], rkv[:, :, 2]
    xv, xw, xa, xg = xm[:, :, 2], xm[:, :, 3], xm[:, :, 4], xm[:, :, 5]
    w_log = -jax.nn.softplus(-(w0 + jnp.tanh(xw @ w1) @ w2)) - 0.5
    if vres is None:
        v_first = v
    else:
        v0, v1, v2 = vres
        v = v + (v_first - v) * jax.nn.sigmoid(v0 + (xv @ v1) @ v2)
    iclr = jax.nn.sigmoid(a0 + (xa @ a1) @ a2)
    g = jax.nn.sigmoid(xg @ g1) @ g2
    hs = (Nb, T, A_HEADS, A_HEAD)
    kk = (k * k_k).astype(f32).reshape(hs)
    kk = kk * lax.rsqrt(jnp.maximum(jnp.sum(kk * kk, axis=-1, keepdims=True), 1e-24))
    k = k * (1 + (iclr - 1) * k_a)
    r4, k4, v4 = (t.astype(f32).reshape(hs) for t in (r, k, v))
    iclr4 = iclr.astype(f32).reshape(hs)
    decay = jnp.exp(-jnp.exp(w_log.astype(f32))).reshape(hs)
    S, o = wkv_scan(S0.astype(f32), r4, decay, k4, v4, -kk, kk * iclr4)
    mu = jnp.mean(o, axis=-1, keepdims=True)
    var = jnp.mean(jnp.square(o - mu), axis=-1, keepdims=True)
    o = ((o - mu) * lax.rsqrt(var + LNX_EPS)).reshape(Nb, T, D) * lnx_w + lnx_b
    bonus = (jnp.sum(r4 * k4 * r_k, axis=-1, keepdims=True) * v4).reshape(Nb, T, D)
    y = ((o + bonus) * g.astype(f32)).astype(h.dtype) @ w_o
    return y, S, h[:, -1], v_first


def swa_project(h, w_qkv, b_qkv):
    Nb, T, _ = h.shape
    qkv = h @ w_qkv + b_qkv
    nq = B_HEADS * B_HEAD
    nk = B_KV_HEADS * B_HEAD
    q = qkv[..., :nq].reshape(Nb, T, B_KV_HEADS, B_GROUP, B_HEAD)
    k = qkv[..., nq:nq + nk].reshape(Nb, T, B_KV_HEADS, B_HEAD)
    v = qkv[..., nq + nk:].reshape(Nb, T, B_KV_HEADS, B_HEAD)
    return q, k, v


def sink_attend(s, mask, sinks, v, eq):
    s = jnp.where(mask, s, -jnp.inf)
    sk = sinks.astype(jnp.float32).reshape(B_KV_HEADS, B_GROUP, 1, 1)
    m = jnp.maximum(jnp.max(s, axis=-1, keepdims=True), sk)
    p = jnp.exp(s - m)
    p = p / (jnp.sum(p, axis=-1, keepdims=True) + jnp.exp(sk - m))
    return jnp.einsum(eq, p.astype(v.dtype), v)


def swa_prompt(h, w_qkv, b_qkv, sinks, w_o, b_o, table, n_rows):
    Nb, T, D = h.shape
    nb = T // WINDOW
    q, k, v = swa_project(h, w_qkv, b_qkv)
    q = q.reshape(Nb, nb, WINDOW, B_KV_HEADS, B_GROUP, B_HEAD)

    def band(t):
        tb = t.reshape(Nb, nb, WINDOW, B_KV_HEADS, B_HEAD)
        prev = jnp.pad(tb, ((0, 0), (1, 0), (0, 0), (0, 0), (0, 0)))[:, :-1]
        return jnp.concatenate([prev, tb], axis=2)
    kw, vw = band(k), band(v)
    qi = jnp.arange(WINDOW)[:, None]
    kj = jnp.arange(2 * WINDOW)[None, :]
    dist = qi + WINDOW - kj
    valid = ((dist >= 0) & (dist < WINDOW))[None] & ((jnp.arange(nb)[:, None, None] > 0) | (kj >= WINDOW)[None])
    s = jnp.einsum('bnqhgd,bnkhd->bnhgqk', q, kw).astype(jnp.float32) * SM_SCALE + rel_bias(dist, table)
    o = sink_attend(s, valid[None, :, None, None], sinks, vw, 'bnhgqk,bnkhd->bnqhgd')
    y = o.reshape(Nb, T, D) @ w_o + b_o
    return y, k[:, -n_rows:], v[:, -n_rows:]


def swa_sample(h, ck, cv, w_qkv, b_qkv, sinks, w_o, b_o, table):
    Nb, L, D = h.shape
    n_rows = ck.shape[1]
    q, k, v = swa_project(h, w_qkv, b_qkv)
    kall = jnp.concatenate([ck.astype(k.dtype), k], axis=1)
    vall = jnp.concatenate([cv.astype(v.dtype), v], axis=1)
    dist = (n_rows + jnp.arange(L)[:, None]) - jnp.arange(n_rows + L)[None, :]
    mask = (dist >= 0) & (dist < WINDOW)
    s = jnp.einsum('blhgd,bkhd->bhglk', q, kall).astype(jnp.float32) * SM_SCALE + rel_bias(dist, table)
    o = sink_attend(s, mask, sinks, vall, 'bhglk,bkhd->blhgd')
    y = o.reshape(Nb, L, D) @ w_o + b_o
    return y, kall[:, -n_rows:], vall[:, -n_rows:]


def moe(h, layer, w_router, b_router, w_gate, b_gate, w_up, b_up, w_down, b_down):
    T, D = h.shape
    logits = (h @ w_router[layer] + b_router[layer]).astype(jnp.float32)
    top_v, top_i = lax.top_k(logits, TOP_K)
    gates = jax.nn.softmax(top_v, axis=-1).astype(h.dtype)
    n_assign = T * TOP_K
    e_flat = top_i.reshape(n_assign)
    tok_flat = jnp.repeat(jnp.arange(T, dtype=jnp.int32), TOP_K)
    order = jnp.argsort(e_flat)
    e_sorted = e_flat[order]
    counts = jnp.bincount(e_flat, length=N_EXPERTS)
    padded = (counts + EXPERT_BLOCK - 1) // EXPERT_BLOCK * EXPERT_BLOCK
    pend = jnp.cumsum(padded)
    slot = (pend - padded)[e_sorted] + jnp.arange(n_assign) - (jnp.cumsum(counts) - counts)[e_sorted]
    n_blk = -(-n_assign // EXPERT_BLOCK) + N_EXPERTS
    n_slots = n_blk * EXPERT_BLOCK
    slot_tok = jnp.full((n_slots,), T, jnp.int32).at[slot].set(tok_flat[order])
    slot_gate = jnp.zeros((n_slots,), h.dtype).at[slot].set(gates.reshape(n_assign)[order])
    blk_expert = jnp.minimum(jnp.searchsorted(pend, jnp.arange(n_blk) * EXPERT_BLOCK, side='right'), N_EXPERTS - 1)
    xs = jnp.concatenate([h, jnp.zeros((1, D), h.dtype)], axis=0)[slot_tok].reshape(n_blk, EXPERT_BLOCK, D)

    def run_block(args):
        xb, e = args
        glu = jnp.minimum(xb @ w_gate[layer, e] + b_gate[layer, e], SWIGLU_LIMIT)
        lin = jnp.clip(xb @ w_up[layer, e] + b_up[layer, e], -SWIGLU_LIMIT, SWIGLU_LIMIT)
        return (glu * jax.nn.sigmoid(SWIGLU_ALPHA * glu) * (lin + 1)) @ w_down[layer, e] + b_down[layer, e]
    ys = lax.map(run_block, (xs, blk_expert)).reshape(n_slots, D) * slot_gate[:, None]
    return jax.ops.segment_sum(ys, slot_tok, num_segments=T + 1)[:T]


def trunk(x, c, wkv0, shift0, cache_k, cache_v, n_rows, P):
    Nb, T, D = x.shape
    v_first = None
    wkv_new, shift_new, k_new, v_new = [], [], [], []
    for i in range(DEPTH):
        mod = jax.nn.silu(c) @ P['w_ada'][i] + P['b_ada'][i]
        sh1, sc1, gt1, sh2, sc2, gt2 = jnp.split(mod[:, None, :], 6, axis=-1)
        h = rms_norm(x, P['norm1_g'][i]) * (1 + sc1) + sh1
        j = i // 2
        if i % 2 == 0:
            vres = None if j == 0 else (P['a_v0'][j - 1], P['a_v1'][j - 1], P['a_v2'][j - 1])
            y, S, last, v_first = rwkv7_mix(
                h, shift0[j], wkv0[j], v_first, vres, P['a_mix'][j], P['a_w_rkv'][j],
                P['a_w0'][j], P['a_w1'][j], P['a_w2'][j], P['a_a0'][j], P['a_a1'][j], P['a_a2'][j],
                P['a_g1'][j], P['a_g2'][j], P['a_k_k'][j], P['a_k_a'][j], P['a_r_k'][j],
                P['a_lnx_w'][j], P['a_lnx_b'][j], P['a_w_o'][j])
            wkv_new.append(S)
            shift_new.append(last)
        else:
            if cache_k is None:
                y, kn, vn = swa_prompt(h, P['b_w_qkv'][j], P['b_b_qkv'][j], P['b_sinks'][j],
                                       P['b_w_o'][j], P['b_b_o'][j], P['rel_bias_table'], n_rows)
            else:
                y, kn, vn = swa_sample(h, cache_k[j], cache_v[j], P['b_w_qkv'][j], P['b_b_qkv'][j],
                                       P['b_sinks'][j], P['b_w_o'][j], P['b_b_o'][j], P['rel_bias_table'])
            k_new.append(kn)
            v_new.append(vn)
        x = x + gt1 * y
        h = rms_norm(x, P['norm2_g'][i]) * (1 + sc2) + sh2
        y = moe(h.reshape(Nb * T, D), i, P['m_w_router'], P['m_b_router'], P['m_w_gate'], P['m_b_gate'],
                P['m_w_up'], P['m_b_up'], P['m_w_down'], P['m_b_down']).reshape(Nb, T, D)
        x = x + gt2 * y
    return rms_norm(x, P['final_g']), jnp.stack(wkv_new), jnp.stack(shift_new), jnp.stack(k_new), jnp.stack(v_new)


def setup_inputs(seed: int = 0) -> dict:
    key = jax.random.key(seed)
    ks = iter(jax.random.split(key, 64))
    f32 = jnp.float32
    D = D_MODEL
    F = D_EXPERT
    cache_rows = min(WINDOW, PAST_LEN)
    qkv_cols = (B_HEADS + 2 * B_KV_HEADS) * B_HEAD

    def nrm(shape, scale=1.0):
        return jax.random.normal(next(ks), shape, f32) * scale

    def uni(shape, lo, hi):
        return jax.random.uniform(next(ks), shape, f32, lo, hi)

    return {
        'x_prompt': nrm((BATCH, SEQ, D)),
        'x_sample': nrm((DEC_BATCH, DEC_SEQ, D)),
        'c_prompt': nrm((BATCH, D)),
        'c_sample': nrm((DEC_BATCH, D)),
        'state_wkv': nrm((N_A, DEC_BATCH, A_HEADS, A_HEAD, A_HEAD), 0.5),
        'state_shift': nrm((N_A, DEC_BATCH, D)),
        'cache_k': nrm((N_B, DEC_BATCH, cache_rows, B_KV_HEADS, B_HEAD)),
        'cache_v': nrm((N_B, DEC_BATCH, cache_rows, B_KV_HEADS, B_HEAD)),
        'w_ada': nrm((DEPTH, D, 6 * D), 0.5 * D ** -0.5),
        'b_ada': nrm((DEPTH, 6 * D), 0.02),
        'norm1_g': 1.0 + nrm((DEPTH, D), 0.1),
        'norm2_g': 1.0 + nrm((DEPTH, D), 0.1),
        'final_g': 1.0 + nrm((D,), 0.1),
        'rel_bias_table': nrm((N_BUCKETS, B_HEADS), 0.5),
        'a_mix': uni((N_A, 6, D), 0.0, 1.0),
        'a_w_rkv': nrm((N_A, 3, D, D), D ** -0.5),
        'a_w0': uni((N_A, D), -6.0, -1.0),
        'a_w1': nrm((N_A, D, LORA_DECAY), D ** -0.5),
        'a_w2': nrm((N_A, LORA_DECAY, D), 0.1 * LORA_DECAY ** -0.5),
        'a_a0': nrm((N_A, D), 0.1),
        'a_a1': nrm((N_A, D, LORA_AAA), D ** -0.5),
        'a_a2': nrm((N_A, LORA_AAA, D), 0.5 * LORA_AAA ** -0.5),
        'a_v0': nrm((N_A - 1, D), 0.1),
        'a_v1': nrm((N_A - 1, D, LORA_MV), D ** -0.5),
        'a_v2': nrm((N_A - 1, LORA_MV, D), 0.5 * LORA_MV ** -0.5),
        'a_g1': nrm((N_A, D, LORA_GATE), D ** -0.5),
        'a_g2': nrm((N_A, LORA_GATE, D), LORA_GATE ** -0.5),
        'a_k_k': 0.85 + nrm((N_A, D), 0.05),
        'a_k_a': 1.0 + nrm((N_A, D), 0.05),
        'a_r_k': nrm((N_A, A_HEADS, A_HEAD), 0.1),
        'a_lnx_w': 1.0 + nrm((N_A, D), 0.1),
        'a_lnx_b': nrm((N_A, D), 0.02),
        'a_w_o': nrm((N_A, D, D), D ** -0.5),
        'b_w_qkv': nrm((N_B, D, qkv_cols), D ** -0.5),
        'b_b_qkv': nrm((N_B, qkv_cols), 0.02),
        'b_sinks': nrm((N_B, B_HEADS), 1.0),
        'b_w_o': nrm((N_B, D, D), D ** -0.5),
        'b_b_o': nrm((N_B, D), 0.02),
        'm_w_router': nrm((DEPTH, D, N_EXPERTS), D ** -0.5),
        'm_b_router': nrm((DEPTH, N_EXPERTS), 0.01),
        'm_w_gate': nrm((DEPTH, N_EXPERTS, D, F), D ** -0.5),
        'm_b_gate': nrm((DEPTH, N_EXPERTS, F), 0.02),
        'm_w_up': nrm((DEPTH, N_EXPERTS, D, F), D ** -0.5),
        'm_b_up': nrm((DEPTH, N_EXPERTS, F), 0.02),
        'm_w_down': nrm((DEPTH, N_EXPERTS, F, D), F ** -0.5),
        'm_b_down': nrm((DEPTH, N_EXPERTS, D), 0.02),
    }


def reference(x_prompt, x_sample, c_prompt, c_sample, state_wkv, state_shift, cache_k, cache_v,
              w_ada, b_ada, norm1_g, norm2_g, final_g, rel_bias_table,
              a_mix, a_w_rkv, a_w0, a_w1, a_w2, a_a0, a_a1, a_a2, a_v0, a_v1, a_v2,
              a_g1, a_g2, a_k_k, a_k_a, a_r_k, a_lnx_w, a_lnx_b, a_w_o,
              b_w_qkv, b_b_qkv, b_sinks, b_w_o, b_b_o,
              m_w_router, m_b_router, m_w_gate, m_b_gate, m_w_up, m_b_up, m_w_down, m_b_down):
    P = dict(w_ada=w_ada, b_ada=b_ada, norm1_g=norm1_g, norm2_g=norm2_g, final_g=final_g,
             rel_bias_table=rel_bias_table, a_mix=a_mix, a_w_rkv=a_w_rkv, a_w0=a_w0, a_w1=a_w1,
             a_w2=a_w2, a_a0=a_a0, a_a1=a_a1, a_a2=a_a2, a_v0=a_v0, a_v1=a_v1, a_v2=a_v2,
             a_g1=a_g1, a_g2=a_g2, a_k_k=a_k_k, a_k_a=a_k_a, a_r_k=a_r_k, a_lnx_w=a_lnx_w,
             a_lnx_b=a_lnx_b, a_w_o=a_w_o, b_w_qkv=b_w_qkv, b_b_qkv=b_b_qkv, b_sinks=b_sinks,
             b_w_o=b_w_o, b_b_o=b_b_o, m_w_router=m_w_router, m_b_router=m_b_router,
             m_w_gate=m_w_gate, m_b_gate=m_b_gate, m_w_up=m_w_up, m_b_up=m_b_up,
             m_w_down=m_w_down, m_b_down=m_b_down)
    n_rows = cache_k.shape[2]
    n_prompt = x_prompt.shape[0]
    wkv0 = jnp.zeros((N_A, n_prompt, A_HEADS, A_HEAD, A_HEAD), jnp.float32)
    shift0 = jnp.zeros((N_A, n_prompt, D_MODEL), x_prompt.dtype)
    y_prompt, wkv_p, shift_p, k_p, v_p = trunk(x_prompt, c_prompt, wkv0, shift0, None, None, n_rows, P)
    y_sample, wkv_s, shift_s, k_s, v_s = trunk(x_sample, c_sample, state_wkv, state_shift, cache_k, cache_v, n_rows, P)
    return (y_prompt, y_sample, wkv_p, shift_p, k_p, v_p, wkv_s, shift_s, k_s, v_s)
```

```python
import functools
import math

import jax
import jax.numpy as jnp
from jax import lax
from jax.experimental import pallas as pl
from jax.experimental.pallas import tpu as pltpu

F32 = jnp.float32
BF16 = jnp.bfloat16

HEAD = 64
WINDOW = 128
GROUP = 8
N_BUCKETS = 32
MAX_DISTANCE = 128
TOP_K = 4
SWIGLU_ALPHA = 1.702
SWIGLU_LIMIT = 7.0
RMS_EPS = 1e-5
LNX_EPS = 64e-5
NEG = -1e30
QUAD = 4 * HEAD
EXPERT_TILE = 256
VMEM_LIMIT = 56 * 1024 * 1024


def _cparams(n_axes):
    return pltpu.CompilerParams(dimension_semantics=("arbitrary",) * n_axes,
                                vmem_limit_bytes=VMEM_LIMIT)


def _tile(n, pref):
    if n <= pref:
        return n
    t = pref
    while n % t:
        t //= 2
    return t


def _dot(a, b):
    return jnp.dot(a.astype(BF16), b.astype(BF16), preferred_element_type=F32)


def _dot_nt(a, b):
    return lax.dot_general(a.astype(BF16), b.astype(BF16), (((1,), (1,)), ((), ())),
                           preferred_element_type=F32)


def _dot_tn(a, b):
    return lax.dot_general(a.astype(BF16), b.astype(BF16), (((0,), (0,)), ((), ())),
                           preferred_element_type=F32)


def _split3(x):
    h1 = x.astype(BF16)
    r1 = x - h1.astype(F32)
    h2 = r1.astype(BF16)
    h3 = (r1 - h2.astype(F32)).astype(BF16)
    return h1, h2, h3


_NN = (((1,), (0,)), ((), ()))
_NT = (((1,), (1,)), ((), ()))
_TN = (((0,), (0,)), ((), ()))


def _hl(x):
    hi = x.astype(BF16)
    return hi, (x - hi.astype(F32)).astype(BF16)


def _dot3(a, b, dims):
    f = lambda x, y: lax.dot_general(x, y, dims, preferred_element_type=F32)
    return f(a[0], b[0]) + (f(a[0], b[1]) + f(a[1], b[0]))


def _dot_exact_rhs01(x, sel):
    s = sel.astype(BF16)
    out = None
    for p in _split3(x):
        t = jnp.dot(p, s, preferred_element_type=F32)
        out = t if out is None else out + t
    return out


def _dot_exact_lhs01(sel, x):
    s = sel.astype(BF16)
    out = None
    for p in _split3(x):
        t = jnp.dot(s, p, preferred_element_type=F32)
        out = t if out is None else out + t
    return out


def _mm_kernel(*refs, pre_act, act, has_bias, has_res):
    x_ref, w_ref = refs[0], refs[1]
    k = 2
    b_ref = res_ref = gate_ref = None
    if has_bias:
        b_ref = refs[k]
        k += 1
    if has_res:
        res_ref, gate_ref = refs[k], refs[k + 1]
        k += 2
    o_ref, wbf_ref = refs[k], refs[k + 1]

    @pl.when(pl.program_id(2) == 0)
    def _():
        wbf_ref[...] = w_ref[...].astype(BF16)

    x = x_ref[...]
    if pre_act == "silu":
        x = x.astype(F32)
        x = x * jax.nn.sigmoid(x)
    acc = jnp.dot(x.astype(BF16), wbf_ref[...], preferred_element_type=F32)
    if has_bias:
        acc = acc + b_ref[...]
    if act == "tanh":
        acc = jnp.tanh(acc)
    elif act == "sigmoid":
        acc = jax.nn.sigmoid(acc)
    if has_res:
        acc = res_ref[...] + gate_ref[...] * acc
    o_ref[...] = acc.astype(o_ref.dtype)


def _rowvec_spec(arr, tm, tn, rows_per_seq):
    if arr.ndim == 3:
        return pl.BlockSpec((None, 1, tn), lambda g, j, i: ((i * tm) // rows_per_seq, 0, j))
    return pl.BlockSpec((tm, tn), lambda g, j, i: (i, j))


def _mm(x, w, wlead=(), bias=None, pre_act=None, act=None, res=None, gate=None,
        rows_per_seq=None, out_dtype=F32, tm=1024, tn=512, groups=1, xoff=0, x_shared=False, name="mm"):
    _, m, kdim = x.shape
    n = w.shape[-1]
    tm = _tile(rows_per_seq if (gate is not None and gate.ndim == 3) else m, tm)
    tn = _tile(n, tn)
    nlead = len(wlead)

    def w_map(g, j, i):
        return tuple(g if s == "g" else s for s in wlead) + (0, j)

    in_specs = [
        pl.BlockSpec((None, tm, kdim), (lambda g, j, i: (xoff, i, 0)) if x_shared else (lambda g, j, i: (xoff + g, i, 0))),
        pl.BlockSpec((None,) * nlead + (kdim, tn), w_map),
    ]
    args = [x, w]
    if bias is not None:
        gb = bias.shape[0]
        in_specs.append(pl.BlockSpec((None, 1, tn), (lambda g, j, i: (g, 0, j)) if gb > 1 else (lambda g, j, i: (0, 0, j))))
        args.append(bias)
    if res is not None:
        in_specs.append(pl.BlockSpec((tm, tn), lambda g, j, i: (i, j)))
        in_specs.append(_rowvec_spec(gate, tm, tn, rows_per_seq))
        args += [res, gate]
    return pl.pallas_call(
        functools.partial(_mm_kernel, pre_act=pre_act, act=act, has_bias=bias is not None,
                          has_res=res is not None),
        grid=(groups, n // tn, m // tm),
        in_specs=in_specs,
        out_specs=pl.BlockSpec((None, tm, tn), lambda g, j, i: (g, i, j)),
        out_shape=jax.ShapeDtypeStruct((groups, m, n), out_dtype),
        scratch_shapes=[pltpu.VMEM((kdim, tn), BF16)],
        compiler_params=_cparams(3),
        name=name,
    )(*args)


def _norm_mod_kernel(x_ref, g_ref, sc_ref, sh_ref, *o_refs):
    x = x_ref[...]
    y = x * lax.rsqrt(jnp.mean(x * x, axis=-1, keepdims=True) + RMS_EPS)
    h = (y * g_ref[...]) * (1.0 + sc_ref[...]) + sh_ref[...]
    for o in o_refs:
        o[...] = h.astype(o.dtype)


def _vec_spec(arr, tm, d, rows_per_seq):
    if arr.ndim == 3:
        return pl.BlockSpec((None, 1, d), lambda i: ((i * tm) // rows_per_seq, 0, 0))
    return pl.BlockSpec((tm, d), lambda i: (i, 0))


def _norm_mod(x, g, sc, sh, rows_per_seq, out_dtypes, tm=512):
    m, d = x.shape
    tm = _tile(rows_per_seq if sc.ndim == 3 else m, tm)
    outs = pl.pallas_call(
        _norm_mod_kernel,
        grid=(m // tm,),
        in_specs=[pl.BlockSpec((tm, d), lambda i: (i, 0)),
                  pl.BlockSpec((1, d), lambda i: (0, 0)),
                  _vec_spec(sc, tm, d, rows_per_seq),
                  _vec_spec(sh, tm, d, rows_per_seq)],
        out_specs=[pl.BlockSpec((tm, d), lambda i: (i, 0)) for _ in out_dtypes],
        out_shape=[jax.ShapeDtypeStruct((m, d), dt) for dt in out_dtypes],
        compiler_params=_cparams(1),
        name="norm_mod",
    )(x, g, sc, sh)
    return outs


def _final_norm_kernel(x_ref, g_ref, o_ref):
    x = x_ref[...]
    y = x * lax.rsqrt(jnp.mean(x * x, axis=-1, keepdims=True) + RMS_EPS)
    o_ref[...] = y * g_ref[...]


def _final_norm(x, g, tm=512):
    m, d = x.shape
    tm = _tile(m, tm)
    return pl.pallas_call(
        _final_norm_kernel,
        grid=(m // tm,),
        in_specs=[pl.BlockSpec((tm, d), lambda i: (i, 0)), pl.BlockSpec((1, d), lambda i: (0, 0))],
        out_specs=pl.BlockSpec((tm, d), lambda i: (i, 0)),
        out_shape=jax.ShapeDtypeStruct((m, d), F32),
        compiler_params=_cparams(1),
        name="final_norm",
    )(x, g)


def _mix_kernel(h_ref, hp_ref, mix_ref, o_ref):
    h = h_ref[...]
    dlt = hp_ref[...] - h
    for j in range(6):
        o_ref[j] = (h + dlt * mix_ref[j:j + 1, :]).astype(BF16)


def _mix(h, hp, mix, tm=256):
    m, d = h.shape
    tm = _tile(m, tm)
    return pl.pallas_call(
        _mix_kernel,
        grid=(m // tm,),
        in_specs=[pl.BlockSpec((tm, d), lambda i: (i, 0)),
                  pl.BlockSpec((tm, d), lambda i: (i, 0)),
                  pl.BlockSpec((6, d), lambda i: (0, 0))],
        out_specs=pl.BlockSpec((6, tm, d), lambda i: (0, i, 0)),
        out_shape=jax.ShapeDtypeStruct((6, m, d), BF16),
        compiler_params=_cparams(1),
        name="token_shift_mix",
    )(h, hp, mix)


def _prep_kernel(kin_ref, vin_ref, vf_ref, tw_ref, ta_ref, tv_ref, tg_ref,
                 w2_ref, a2_ref, v2_ref, g2_ref, w0_ref, a0_ref, v0_ref, kk_ref, ka_ref,
                 e_ref, et_ref,
                 lw_ref, k_ref, v_ref, a_ref, b_ref, g_ref, *, has_vres):
    k = kin_ref[...]
    v = vin_ref[...]
    z = -(w0_ref[...] + _dot(tw_ref[...], w2_ref[...]))
    softplus = jnp.maximum(z, 0.0) + jnp.log(1.0 + jnp.exp(-jnp.abs(z)))
    w_log = -softplus - 0.5
    lw_ref[...] = -jnp.exp(w_log)
    iclr = jax.nn.sigmoid(a0_ref[...] + _dot(ta_ref[...], a2_ref[...]))
    if has_vres:
        vg = jax.nn.sigmoid(v0_ref[...] + _dot(tv_ref[...], v2_ref[...]))
        v = v + (vf_ref[...] - v) * vg
    v_ref[...] = v
    g_ref[...] = _dot(tg_ref[...], g2_ref[...])
    kk = k * kk_ref[...]
    ss = _dot_exact_rhs01(kk * kk, e_ref[...])
    inv = lax.rsqrt(jnp.maximum(ss, 1e-24))
    kk = kk * _dot_exact_rhs01(inv, et_ref[...])
    k_ref[...] = k * (1.0 + (iclr - 1.0) * ka_ref[...])
    a_ref[...] = -kk
    b_ref[...] = kk * iclr


def _head_selectors(d):
    nh = d // HEAD
    e = (jnp.arange(d)[:, None] // HEAD == jnp.arange(nh)[None, :]).astype(F32)
    return e, e.T


def _prep(rkv, vf, tw, ta, tv, tg, w2, a2, v2, g2, w0, a0, v0, k_k, k_a, has_vres, tm=128):
    _, m, d = rkv.shape
    tm = _tile(m, tm)
    e, et = _head_selectors(d)
    row = lambda a: pl.BlockSpec((tm, a.shape[-1]), lambda i: (i, 0))
    full = lambda a: pl.BlockSpec(a.shape, lambda i: (0,) * a.ndim)
    vecs = [w0, a0, v0, k_k, k_a]
    outs = pl.pallas_call(
        functools.partial(_prep_kernel, has_vres=has_vres),
        grid=(m // tm,),
        in_specs=[pl.BlockSpec((None, tm, d), lambda i: (1, i, 0)), pl.BlockSpec((None, tm, d), lambda i: (2, i, 0)),
                  row(vf), row(tw), row(ta), row(tv), row(tg),
                  full(w2), full(a2), full(v2), full(g2)] + [full(x) for x in vecs] + [full(e), full(et)],
        out_specs=[pl.BlockSpec((tm, d), lambda i: (i, 0))] * 6,
        out_shape=[jax.ShapeDtypeStruct((m, d), F32)] * 6,
        compiler_params=_cparams(1),
        name="rwkv_prep",
    )(rkv, rkv, vf, tw, ta, tv, tg, w2, a2, v2, g2, *vecs, e, et)
    return outs


def _wkv_kernel(r_ref, lw_ref, k_ref, v_ref, a_ref, b_ref, s0_ref, o_ref, sout_ref,
                s_scr, at_scr, rt_scr, kh_scr, bh_scr, kp_scr, bp_scr, v_scr, oq_scr, dg_scr, *, chunk, nquad):
    c = pl.program_id(1)
    rows = 4 * chunk

    @pl.when(c == 0)
    def _():
        s0 = s0_ref[...]
        lane_head = lax.broadcasted_iota(jnp.int32, (HEAD, QUAD), 1) // HEAD
        for q in range(nquad):
            slab = s0[:, q * QUAD:(q + 1) * QUAD]
            s_scr[q] = jnp.concatenate(
                [jnp.where(lane_head == h, slab, 0.0) for h in range(4)], axis=0)

    lw = lw_ref[...]
    ti = lax.broadcasted_iota(jnp.int32, (chunk, chunk), 0)
    tj = lax.broadcasted_iota(jnp.int32, (chunk, chunk), 1)
    cum = _dot_exact_lhs01(ti >= tj, lw)
    cin = jnp.exp(cum)
    inv = jnp.exp(-cum)
    tail = jnp.exp(cum[chunk - 1:chunk, :] - cum)
    a = a_ref[...]
    b = b_ref[...]
    k = k_ref[...]
    v = v_ref[...]
    staged = ((at_scr, a * jnp.exp(cum - lw)), (rt_scr, r_ref[...] * cin), (kh_scr, k * inv),
              (bh_scr, b * inv), (kp_scr, k * tail), (bp_scr, b * tail), (v_scr, v),
              (dg_scr, cin[chunk - 1:chunk, :]))
    for scr, val in staged:
        for q in range(nquad):
            scr[q] = val[:, q * QUAD:(q + 1) * QUAD]

    ri = lax.broadcasted_iota(jnp.int32, (rows, rows), 0)
    rj = lax.broadcasted_iota(jnp.int32, (rows, rows), 1)
    same = (ri // chunk) == (rj // chunk)
    tril_s = same & ((ri % chunk) > (rj % chunk))
    tril_i = same & ((ri % chunk) >= (rj % chunk))
    eye_r = (ri == rj).astype(F32)
    lane_head = lax.broadcasted_iota(jnp.int32, (chunk, QUAD), 1) // HEAD
    qi = lax.broadcasted_iota(jnp.int32, (QUAD, QUAD), 0)
    qj = lax.broadcasted_iota(jnp.int32, (QUAD, QUAD), 1)

    def bd(x):
        return jnp.concatenate([jnp.where(lane_head == h, x, 0.0) for h in range(4)], axis=0)

    def quad_body(q, carry):
        xr_f = bd(rt_scr[q])
        xa = _hl(bd(at_scr[q]))
        xr = _hl(xr_f)
        yk = _hl(bd(kh_scr[q]))
        yb = _hl(bd(bh_scr[q]))
        vb = _hl(bd(v_scr[q]))
        kp = _hl(bd(kp_scr[q]))
        bp = _hl(bd(bp_scr[q]))
        a_ak = _hl(jnp.where(tril_s, _dot3(xa, yk, _NT), 0.0))
        a_ab = jnp.where(tril_s, _dot3(xa, yb, _NT), 0.0)
        a_rk = _hl(jnp.where(tril_i, _dot3(xr, yk, _NT), 0.0))
        a_rb = _hl(jnp.where(tril_i, _dot3(xr, yb, _NT), 0.0))
        tinv = eye_r + a_ab
        p = a_ab
        n = 2
        while n < chunk:
            ph = _hl(p)
            p = _dot3(ph, ph, _NN)
            tinv = tinv + _dot3(_hl(tinv), _hl(p), _NN)
            n *= 2
        tinv = _hl(tinv)
        av = _hl(_dot3(a_ak, vb, _NN))
        ap = _hl(_dot3(tinv, xa, _NN))
        u0 = _hl(_dot3(tinv, av, _NN))
        rp = _hl(xr_f + _dot3(a_rb, ap, _NN))
        o0 = _dot3(a_rk, vb, _NN) + _dot3(a_rb, u0, _NN)
        dg = jnp.where(qi == qj, dg_scr[q], 0.0)
        mm = _hl(dg + _dot3(ap, bp, _TN))
        sadd = _dot3(vb, kp, _TN) + _dot3(u0, bp, _TN)
        s = _hl(s_scr[q])
        o = _dot3(rp, s, _NT) + o0
        s_new = _dot3(s, mm, _NN) + sadd
        s_scr[q] = s_new
        oq_scr[q] = o[0:chunk] + o[chunk:2 * chunk] + o[2 * chunk:3 * chunk] + o[3 * chunk:4 * chunk]
        return carry

    lax.fori_loop(0, nquad, quad_body, 0)
    for q in range(nquad):
        o_ref[:, q * QUAD:(q + 1) * QUAD] = oq_scr[q]

    @pl.when(c == pl.num_programs(1) - 1)
    def _():
        for q in range(nquad):
            sq = s_scr[q]
            sout_ref[:, q * QUAD:(q + 1) * QUAD] = (sq[0:HEAD] + sq[HEAD:2 * HEAD]
                                                    + sq[2 * HEAD:3 * HEAD] + sq[3 * HEAD:4 * HEAD])


def _wkv(r, lw, k, v, a, b, s0, chunk):
    bsz, t, d = r.shape
    nquad = d // QUAD
    seq = pl.BlockSpec((None, chunk, d), lambda bi, ci: (bi, ci, 0))
    st = pl.BlockSpec((None, HEAD, d), lambda bi, ci: (bi, 0, 0))
    cd = pltpu.VMEM((nquad, chunk, QUAD), F32)
    return pl.pallas_call(
        functools.partial(_wkv_kernel, chunk=chunk, nquad=nquad),
        grid=(bsz, t // chunk),
        in_specs=[seq] * 6 + [st],
        out_specs=[seq, st],
        out_shape=[jax.ShapeDtypeStruct((bsz, t, d), F32), jax.ShapeDtypeStruct((bsz, HEAD, d), F32)],
        scratch_shapes=[pltpu.VMEM((nquad, QUAD, QUAD), F32), cd, cd, cd, cd, cd, cd, cd, cd,
                        pltpu.VMEM((nquad, 1, QUAD), F32)],
        compiler_params=_cparams(2),
        name="wkv_scan",
    )(r, lw, k, v, a, b, s0)


def _post_kernel(o_ref, rkv_ref, k_ref, v_ref, g_ref, lw_ref, lb_ref, rk_ref, e_ref, et_ref, y_ref):
    o = o_ref[...]
    e = e_ref[...]
    et = et_ref[...]
    inv_n = 1.0 / HEAD
    mu = _dot_exact_rhs01(_dot_exact_rhs01(o, e) * inv_n, et)
    dlt = o - mu
    var = _dot_exact_rhs01(dlt * dlt, e) * inv_n
    rstd = _dot_exact_rhs01(lax.rsqrt(var + LNX_EPS), et)
    on = dlt * rstd * lw_ref[...] + lb_ref[...]
    rks = _dot_exact_rhs01(rkv_ref[...] * k_ref[...] * rk_ref[...], e)
    bonus = _dot_exact_rhs01(rks, et) * v_ref[...]
    y_ref[...] = ((on + bonus) * g_ref[...]).astype(BF16)


def _post(o, rkv, k, v, g, lnx_w, lnx_b, r_k, tm=128):
    m, d = o.shape
    tm = _tile(m, tm)
    e, et = _head_selectors(d)
    row = pl.BlockSpec((tm, d), lambda i: (i, 0))
    full = lambda a: pl.BlockSpec(a.shape, lambda i: (0,) * a.ndim)
    return pl.pallas_call(
        _post_kernel,
        grid=(m // tm,),
        in_specs=[row, pl.BlockSpec((None, tm, d), lambda i: (0, i, 0)), row, row, row,
                  full(lnx_w), full(lnx_b), full(r_k), full(e), full(et)],
        out_specs=row,
        out_shape=jax.ShapeDtypeStruct((m, d), BF16),
        compiler_params=_cparams(1),
        name="rwkv_post",
    )(o, rkv, k, v, g, lnx_w, lnx_b, r_k, e, et)


def _t5_bucket(dist):
    max_exact = N_BUCKETS // 2
    n = jnp.maximum(dist, 0)
    nf = jnp.maximum(n, max_exact).astype(F32)
    large = max_exact + (jnp.log(nf / max_exact) / math.log(MAX_DISTANCE / max_exact)
                         * (N_BUCKETS - max_exact)).astype(jnp.int32)
    return jnp.where(n < max_exact, n, jnp.minimum(large, N_BUCKETS - 1))


def _masked_bias(dist, table):
    bias = jnp.transpose(table[_t5_bucket(dist)], (2, 0, 1)).astype(F32)
    valid = (dist >= 0) & (dist < WINDOW)
    return jnp.where(valid[None], bias, NEG)


def _swa_prompt_kernel(sink_ref, q_ref, kp_ref, kc_ref, vp_ref, vc_ref, bias_ref, o_ref, *, nheads, scale):
    first = pl.program_id(1) == 0
    col = lax.broadcasted_iota(jnp.int32, (WINDOW, 2 * WINDOW), 1)
    hide_prev = first & (col < WINDOW)
    for g in range(nheads // GROUP):
        ks = slice(g * HEAD, (g + 1) * HEAD)
        kw = jnp.concatenate([kp_ref[:, ks], kc_ref[:, ks]], axis=0).astype(BF16)
        vw = jnp.concatenate([vp_ref[:, ks], vc_ref[:, ks]], axis=0).astype(BF16)
        for hh in range(GROUP):
            h = g * GROUP + hh
            qh = q_ref[:, h * HEAD:(h + 1) * HEAD]
            s = _dot_nt(qh, kw) * scale + bias_ref[h]
            s = jnp.where(hide_prev, NEG, s)
            sk = sink_ref[h]
            m = jnp.maximum(jnp.max(s, axis=-1, keepdims=True), sk)
            p = jnp.exp(s - m)
            p = p / (jnp.sum(p, axis=-1, keepdims=True) + jnp.exp(sk - m))
            o_ref[:, h * HEAD:(h + 1) * HEAD] = _dot(p, vw).astype(o_ref.dtype)


def _swa_prompt(qkv, sinks, bias, nheads):
    bsz, t, _ = qkv.shape
    d = nheads * HEAD
    kvw = (nheads // GROUP) * HEAD
    nb = t // WINDOW
    kblk = d // kvw
    qspec = pl.BlockSpec((None, WINDOW, d), lambda b, n, s: (b, n, 0))
    kprev = pl.BlockSpec((None, WINDOW, kvw), lambda b, n, s: (b, jnp.maximum(n - 1, 0), kblk))
    kcur = pl.BlockSpec((None, WINDOW, kvw), lambda b, n, s: (b, n, kblk))
    vprev = pl.BlockSpec((None, WINDOW, kvw), lambda b, n, s: (b, jnp.maximum(n - 1, 0), kblk + 1))
    vcur = pl.BlockSpec((None, WINDOW, kvw), lambda b, n, s: (b, n, kblk + 1))
    bspec = pl.BlockSpec(bias.shape, lambda b, n, s: (0, 0, 0))
    return pl.pallas_call(
        functools.partial(_swa_prompt_kernel, nheads=nheads, scale=HEAD ** -0.5),
        grid_spec=pltpu.PrefetchScalarGridSpec(
            num_scalar_prefetch=1,
            grid=(bsz, nb),
            in_specs=[qspec, kprev, kcur, vprev, vcur, bspec],
            out_specs=pl.BlockSpec((None, WINDOW, d), lambda b, n, s: (b, n, 0)),
        ),
        out_shape=jax.ShapeDtypeStruct((bsz, t, d), BF16),
        compiler_params=_cparams(2),
        name="swa_prompt",
    )(sinks, qkv, qkv, qkv, qkv, qkv, bias)


def _swa_sample_kernel(sink_ref, q_ref, kn_ref, vn_ref, ck_ref, cv_ref, bc_ref, bn_ref, o_ref, *, nheads, scale):
    for g in range(nheads // GROUP):
        ks = slice(g * HEAD, (g + 1) * HEAD)
        kc = ck_ref[:, ks].astype(BF16)
        vc = cv_ref[:, ks].astype(BF16)
        kn = kn_ref[:, ks].astype(BF16)
        vn = vn_ref[:, ks].astype(BF16)
        for hh in range(GROUP):
            h = g * GROUP + hh
            qh = q_ref[:, h * HEAD:(h + 1) * HEAD]
            sc = _dot_nt(qh, kc) * scale + bc_ref[h]
            sn = _dot_nt(qh, kn) * scale + bn_ref[h]
            sk = sink_ref[h]
            m = jnp.maximum(jnp.maximum(jnp.max(sc, axis=-1, keepdims=True),
                                        jnp.max(sn, axis=-1, keepdims=True)), sk)
            pc = jnp.exp(sc - m)
            pn = jnp.exp(sn - m)
            den = jnp.sum(pc, axis=-1, keepdims=True) + jnp.sum(pn, axis=-1, keepdims=True) + jnp.exp(sk - m)
            o = _dot(pc / den, vc) + _dot(pn / den, vn)
            o_ref[:, h * HEAD:(h + 1) * HEAD] = o.astype(o_ref.dtype)


def _swa_sample(qkv, ck, cv, lyr, sinks, bias_c, bias_n, nheads):
    bsz, l, _ = qkv.shape
    d = nheads * HEAD
    kvw = (nheads // GROUP) * HEAD
    kblk = d // kvw
    nrows = ck.shape[2]
    full = lambda a: pl.BlockSpec(a.shape, lambda b, s: (0,) * a.ndim)
    return pl.pallas_call(
        functools.partial(_swa_sample_kernel, nheads=nheads, scale=HEAD ** -0.5),
        grid_spec=pltpu.PrefetchScalarGridSpec(
            num_scalar_prefetch=1,
            grid=(bsz,),
            in_specs=[pl.BlockSpec((None, l, d), lambda b, s: (b, 0, 0)),
                      pl.BlockSpec((None, l, kvw), lambda b, s: (b, 0, kblk)),
                      pl.BlockSpec((None, l, kvw), lambda b, s: (b, 0, kblk + 1)),
                      pl.BlockSpec((None, None, nrows, kvw), lambda b, s: (lyr, b, 0, 0)),
                      pl.BlockSpec((None, None, nrows, kvw), lambda b, s: (lyr, b, 0, 0)),
                      full(bias_c), full(bias_n)],
            out_specs=pl.BlockSpec((None, l, d), lambda b, s: (b, 0, 0)),
        ),
        out_shape=jax.ShapeDtypeStruct((bsz, l, d), BF16),
        compiler_params=_cparams(1),
        name="swa_sample",
    )(sinks, qkv, qkv, qkv, ck, cv, bias_c, bias_n)


def _router_kernel(h_ref, w_ref, b_ref, idx_ref, gate_ref, rank_ref, cnt_ref, run_scr, *, n_exp):
    i = pl.program_id(0)
    tm = h_ref.shape[0]

    @pl.when(i == 0)
    def _():
        run_scr[...] = jnp.zeros_like(run_scr)

    logits = _dot(h_ref[...], w_ref[...]) + b_ref[...]
    lane = lax.broadcasted_iota(jnp.int32, (tm, n_exp), 1)
    out_lane = lax.broadcasted_iota(jnp.int32, (tm, 128), 1)
    ri = lax.broadcasted_iota(jnp.int32, (tm, tm), 0)
    rj = lax.broadcasted_iota(jnp.int32, (tm, tm), 1)
    work = logits
    vals, sels = [], []
    chosen = jnp.zeros((tm, n_exp), F32)
    idx_out = jnp.zeros((tm, 128), jnp.int32)
    for kk in range(TOP_K):
        mx = jnp.max(work, axis=-1, keepdims=True)
        idx = jnp.min(jnp.where(work == mx, lane, n_exp), axis=-1, keepdims=True)
        sel = lane == idx
        vals.append(mx)
        sels.append(sel)
        chosen = chosen + sel.astype(F32)
        idx_out = jnp.where(out_lane == kk, idx, idx_out)
        work = jnp.where(sel, -jnp.inf, work)
    es = [jnp.exp(v - vals[0]) for v in vals]
    den = es[0] + es[1] + es[2] + es[3]
    before = jnp.dot((ri > rj).astype(BF16), chosen.astype(BF16), preferred_element_type=F32)
    pos = before + run_scr[...]
    gate_out = jnp.zeros((tm, 128), F32)
    rank_out = jnp.zeros((tm, 128), jnp.int32)
    for kk in range(TOP_K):
        gate_out = jnp.where(out_lane == kk, es[kk] / den, gate_out)
        rk = jnp.sum(jnp.where(sels[kk], pos, 0.0), axis=-1, keepdims=True).astype(jnp.int32)
        rank_out = jnp.where(out_lane == kk, rk, rank_out)
    run_scr[...] = run_scr[...] + jnp.sum(chosen, axis=0, keepdims=True)
    idx_ref[...] = idx_out
    gate_ref[...] = gate_out
    rank_ref[...] = rank_out
    cnt_ref[...] = run_scr[...]


def _router(hb, w_router, b_router, lyr, tm=256):
    t, d = hb.shape
    n_exp = w_router.shape[-1]
    tm = _tile(t, tm)
    wide = pl.BlockSpec((tm, 128), lambda i: (i, 0))
    return pl.pallas_call(
        functools.partial(_router_kernel, n_exp=n_exp),
        grid=(t // tm,),
        in_specs=[pl.BlockSpec((tm, d), lambda i: (i, 0)),
                  pl.BlockSpec((None, d, n_exp), lambda i: (lyr, 0, 0)),
                  pl.BlockSpec((None, 1, n_exp), lambda i: (lyr, 0, 0))],
        out_specs=[wide, wide, wide, pl.BlockSpec((1, n_exp), lambda i: (0, 0))],
        out_shape=[jax.ShapeDtypeStruct((t, 128), jnp.int32), jax.ShapeDtypeStruct((t, 128), F32),
                   jax.ShapeDtypeStruct((t, 128), jnp.int32), jax.ShapeDtypeStruct((1, n_exp), F32)],
        scratch_shapes=[pltpu.VMEM((1, n_exp), F32)],
        compiler_params=_cparams(1),
        name="moe_router",
    )(hb, w_router, b_router)


def _expert_up_kernel(te_ref, nu_ref, x_ref, wg_ref, wu_ref, bg_ref, bu_ref, o_ref, wg_bf, wu_bf):
    i = pl.program_id(1)
    e = te_ref[i]
    prev = te_ref[jnp.maximum(i - 1, 0)]

    @pl.when((i == 0) | (e != prev))
    def _():
        wg_bf[...] = wg_ref[...].astype(BF16)
        wu_bf[...] = wu_ref[...].astype(BF16)

    @pl.when(i < nu_ref[0])
    def _():
        x = x_ref[...]
        glu = jnp.minimum(jnp.dot(x, wg_bf[...], preferred_element_type=F32) + bg_ref[...], SWIGLU_LIMIT)
        lin = jnp.clip(jnp.dot(x, wu_bf[...], preferred_element_type=F32) + bu_ref[...],
                       -SWIGLU_LIMIT, SWIGLU_LIMIT)
        o_ref[...] = (glu * jax.nn.sigmoid(SWIGLU_ALPHA * glu) * (lin + 1.0)).astype(o_ref.dtype)


def _expert_up(tile_e, n_used, xs, w_gate, w_up, b_gate, b_up, lyr, tf=512):
    n_slots, d = xs.shape
    f = w_gate.shape[-1]
    tf = _tile(f, tf)
    n_tiles = n_slots // EXPERT_TILE
    wspec = pl.BlockSpec((None, None, d, tf), lambda j, i, te, nu: (lyr, te[i], 0, j))
    bspec = pl.BlockSpec((None, None, 1, tf), lambda j, i, te, nu: (lyr, te[i], 0, j))
    return pl.pallas_call(
        _expert_up_kernel,
        grid_spec=pltpu.PrefetchScalarGridSpec(
            num_scalar_prefetch=2,
            grid=(f // tf, n_tiles),
            in_specs=[pl.BlockSpec((EXPERT_TILE, d), lambda j, i, te, nu: (i, 0)), wspec, wspec, bspec, bspec],
            out_specs=pl.BlockSpec((EXPERT_TILE, tf), lambda j, i, te, nu: (i, j)),
            scratch_shapes=[pltpu.VMEM((d, tf), BF16), pltpu.VMEM((d, tf), BF16)],
        ),
        out_shape=jax.ShapeDtypeStruct((n_slots, f), BF16),
        compiler_params=_cparams(2),
        name="moe_expert_up",
    )(tile_e, n_used, xs, w_gate, w_up, b_gate, b_up)


def _expert_down_kernel(te_ref, nu_ref, h_ref, wd_ref, bd_ref, gate_ref, o_ref, wd_bf):
    i = pl.program_id(1)
    e = te_ref[i]
    prev = te_ref[jnp.maximum(i - 1, 0)]

    @pl.when((i == 0) | (e != prev))
    def _():
        wd_bf[...] = wd_ref[...].astype(BF16)

    @pl.when(i < nu_ref[0])
    def _():
        y = jnp.dot(h_ref[...], wd_bf[...], preferred_element_type=F32) + bd_ref[...]
        o_ref[...] = y * gate_ref[...]


def _expert_down(tile_e, n_used, hs, w_down, b_down, slot_gate, lyr, tn=512):
    n_slots, f = hs.shape
    d = w_down.shape[-1]
    tn = _tile(d, tn)
    n_tiles = n_slots // EXPERT_TILE
    return pl.pallas_call(
        _expert_down_kernel,
        grid_spec=pltpu.PrefetchScalarGridSpec(
            num_scalar_prefetch=2,
            grid=(d // tn, n_tiles),
            in_specs=[pl.BlockSpec((EXPERT_TILE, f), lambda j, i, te, nu: (i, 0)),
                      pl.BlockSpec((None, None, f, tn), lambda j, i, te, nu: (lyr, te[i], 0, j)),
                      pl.BlockSpec((None, None, 1, tn), lambda j, i, te, nu: (lyr, te[i], 0, j)),
                      pl.BlockSpec((EXPERT_TILE, 1), lambda j, i, te, nu: (i, 0))],
            out_specs=pl.BlockSpec((EXPERT_TILE, tn), lambda j, i, te, nu: (i, j)),
            scratch_shapes=[pltpu.VMEM((f, tn), BF16)],
        ),
        out_shape=jax.ShapeDtypeStruct((n_slots, d), F32),
        compiler_params=_cparams(2),
        name="moe_expert_down",
    )(tile_e, n_used, hs, w_down, b_down, slot_gate)


def _gated_add_kernel(x_ref, y_ref, g_ref, o_ref):
    o_ref[...] = x_ref[...] + g_ref[...] * y_ref[...]


def _moe(hb, lyr, P):
    t, d = hb.shape
    n_exp = P["m_w_router"].shape[-1]
    idx, gate, rank, cnt = _router(hb, P["m_w_router"], P["m_b_router"].reshape(-1, 1, n_exp), lyr)
    e_tk = idx[:, :TOP_K]
    gates = gate[:, :TOP_K]
    counts = cnt[0].astype(jnp.int32)
    padded = (counts + EXPERT_TILE - 1) // EXPERT_TILE * EXPERT_TILE
    pend = jnp.cumsum(padded)
    slot = (pend - padded)[e_tk] + rank[:, :TOP_K]
    n_tiles = -(-(t * TOP_K) // EXPERT_TILE) + n_exp
    n_slots = n_tiles * EXPERT_TILE
    tok = jnp.broadcast_to(jnp.arange(t, dtype=jnp.int32)[:, None], (t, TOP_K))
    slot_tok = jnp.full((n_slots,), t, jnp.int32).at[slot.reshape(-1)].set(tok.reshape(-1))
    slot_gate = jnp.zeros((n_slots,), F32).at[slot.reshape(-1)].set(gates.reshape(-1))
    tile_e = jnp.minimum(jnp.searchsorted(pend, jnp.arange(n_tiles, dtype=jnp.int32) * EXPERT_TILE, side="right"),
                         n_exp - 1).astype(jnp.int32)
    n_used = (pend[-1:] // EXPERT_TILE).astype(jnp.int32)
    xs = jnp.concatenate([hb, jnp.zeros((1, d), hb.dtype)], axis=0)[slot_tok]
    f = P["m_w_gate"].shape[-1]
    hs = _expert_up(tile_e, n_used, xs, P["m_w_gate"], P["m_w_up"],
                    P["m_b_gate"].reshape(-1, n_exp, 1, f), P["m_b_up"].reshape(-1, n_exp, 1, f), lyr)
    ys = _expert_down(tile_e, n_used, hs, P["m_w_down"], P["m_b_down"].reshape(-1, n_exp, 1, d),
                      slot_gate[:, None], lyr)
    return jnp.sum(ys[slot], axis=1)


def _gated_add(x, y, gate, rows_per_seq, tm=512):
    m, d = x.shape
    tm = _tile(rows_per_seq if gate.ndim == 3 else m, tm)
    row = pl.BlockSpec((tm, d), lambda i: (i, 0))
    return pl.pallas_call(
        _gated_add_kernel,
        grid=(m // tm,),
        in_specs=[row, row, _vec_spec(gate, tm, d, rows_per_seq)],
        out_specs=row,
        out_shape=jax.ShapeDtypeStruct((m, d), F32),
        compiler_params=_cparams(1),
        name="gated_add",
    )(x, y, gate)


class _Group:
    def __init__(self, x, per_row_mod):
        self.n, self.t, self.d = x.shape
        self.x = x.reshape(self.n * self.t, self.d)
        self.per_row = per_row_mod

    def vec(self, v):
        if self.per_row:
            return jnp.repeat(v, self.t, axis=0)
        return v[:, None, :]


def _rwkv_layer(grp, h, shift0, s0, v_first, j, P, chunk):
    m, d = h.shape
    h3 = h.reshape(grp.n, grp.t, d)
    hp = jnp.concatenate([shift0[:, None, :], h3[:, :-1]], axis=1).reshape(m, d)
    xm = _mix(h, hp, P["a_mix"][j])
    rkv = _mm(xm, P["a_w_rkv"], wlead=(j, "g"), groups=3)
    tw = _mm(xm, P["a_w1"], wlead=(j,), xoff=3, act="tanh", out_dtype=BF16)[0]
    ta = _mm(xm, P["a_a1"], wlead=(j,), xoff=4, out_dtype=BF16)[0]
    tg = _mm(xm, P["a_g1"], wlead=(j,), xoff=5, act="sigmoid", out_dtype=BF16)[0]
    has_vres = j > 0
    if has_vres:
        tv = _mm(xm, P["a_v1"], wlead=(j - 1,), xoff=2, out_dtype=BF16)[0]
        v2, v0, vf = P["a_v2"][j - 1], P["a_v0"][j - 1][None], v_first
    else:
        tv = jnp.zeros((m, 8), BF16)
        v2, v0, vf = jnp.zeros((8, d), F32), jnp.zeros((1, d), F32), jnp.zeros((m, 8), F32)
    lw, k, v, a, b, g = _prep(rkv, vf, tw, ta, tv, tg, P["a_w2"][j], P["a_a2"][j], v2, P["a_g2"][j],
                              P["a_w0"][j][None], P["a_a0"][j][None], v0,
                              P["a_k_k"][j][None], P["a_k_a"][j][None], has_vres)
    if not has_vres:
        v_first = rkv[2]
    t_pad = -(-grp.t // chunk) * chunk

    def seq(z):
        z = z.reshape(grp.n, grp.t, d)
        if t_pad != grp.t:
            z = jnp.pad(z, ((0, 0), (0, t_pad - grp.t), (0, 0)))
        return z

    o, s_out = _wkv(seq(rkv[0]), seq(lw), seq(k), seq(v), seq(a), seq(b), s0, chunk)
    o = o[:, :grp.t].reshape(m, d)
    y_in = _post(o, rkv, k, v, g, P["a_lnx_w"][j][None], P["a_lnx_b"][j][None],
                 P["a_r_k"][j].reshape(1, d))
    return y_in, s_out, h3[:, -1], v_first


def _state_to_slab(s):
    bsz, nh, hv, hk = s.shape
    return jnp.transpose(s, (0, 2, 1, 3)).reshape(bsz, hv, nh * hk)


def _slab_to_state(s, nh):
    bsz, hv, _ = s.shape
    return jnp.transpose(s.reshape(bsz, hv, nh, HEAD), (0, 2, 1, 3))


def kernel(x_prompt, x_sample, c_prompt, c_sample, state_wkv, state_shift, cache_k, cache_v, w_ada, b_ada, norm1_g, norm2_g, final_g, rel_bias_table, a_mix, a_w_rkv, a_w0, a_w1, a_w2, a_a0, a_a1, a_a2, a_v0, a_v1, a_v2, a_g1, a_g2, a_k_k, a_k_a, a_r_k, a_lnx_w, a_lnx_b, a_w_o, b_w_qkv, b_b_qkv, b_sinks, b_w_o, b_b_o, m_w_router, m_b_router, m_w_gate, m_b_gate, m_w_up, m_b_up, m_w_down, m_b_down):
    P = dict(a_mix=a_mix, a_w_rkv=a_w_rkv, a_w0=a_w0, a_w1=a_w1, a_w2=a_w2, a_a0=a_a0, a_a1=a_a1,
             a_a2=a_a2, a_v0=a_v0, a_v1=a_v1, a_v2=a_v2, a_g1=a_g1, a_g2=a_g2, a_k_k=a_k_k,
             a_k_a=a_k_a, a_r_k=a_r_k, a_lnx_w=a_lnx_w, a_lnx_b=a_lnx_b,
             m_w_router=m_w_router, m_b_router=m_b_router, m_w_gate=m_w_gate, m_b_gate=m_b_gate,
             m_w_up=m_w_up, m_b_up=m_b_up, m_w_down=m_w_down, m_b_down=m_b_down)
    depth = w_ada.shape[0]
    d = x_prompt.shape[-1]
    nheads = d // HEAD
    n_p, n_s = x_prompt.shape[0], x_sample.shape[0]
    n_rows = cache_k.shape[2]
    kvw = (nheads // GROUP) * HEAD

    gp = _Group(x_prompt, per_row_mod=False)
    gs = _Group(x_sample, per_row_mod=True)
    groups = (gp, gs)

    n_c = n_p + n_s
    c_all = jnp.concatenate([c_prompt, c_sample], axis=0)
    c_all = jnp.pad(c_all, ((0, -n_c % 8), (0, 0)))[None]
    mod = _mm(c_all, w_ada, wlead=("g",), bias=b_ada[:, None, :], pre_act="silu", groups=depth, x_shared=True,
              tn=1024)

    qi = jnp.arange(WINDOW)[:, None]
    kj = jnp.arange(2 * WINDOW)[None, :]
    bias_p = _masked_bias(qi + WINDOW - kj, rel_bias_table)
    l_s = gs.t
    dist_s = (n_rows + jnp.arange(l_s)[:, None]) - jnp.arange(n_rows + l_s)[None, :]
    bias_s = _masked_bias(dist_s, rel_bias_table)
    bias_sc, bias_sn = bias_s[:, :, :n_rows], bias_s[:, :, n_rows:]

    s0 = {0: jnp.zeros((state_wkv.shape[0], n_p, HEAD, d), F32),
          1: jax.vmap(_state_to_slab)(state_wkv)}
    shift0 = {0: jnp.zeros((state_shift.shape[0], n_p, d), F32), 1: state_shift}
    ck = cache_k.reshape(cache_k.shape[0], n_s, n_rows, kvw)
    cv = cache_v.reshape(cache_v.shape[0], n_s, n_rows, kvw)
    chunks = {0: min(64, gp.t), 1: 16}

    v_first = {0: None, 1: None}
    wkv_new = {0: [], 1: []}
    shift_new = {0: [], 1: []}
    k_new = {0: [], 1: []}
    v_new = {0: [], 1: []}

    for i in range(depth):
        j = i // 2
        mods = []
        for gi, grp in enumerate(groups):
            lo = 0 if gi == 0 else n_p
            mg = mod[i, lo:lo + grp.n]
            mods.append([grp.vec(z) for z in jnp.split(mg, 6, axis=-1)])
        for gi, grp in enumerate(groups):
            sh1, sc1, gt1 = mods[gi][0], mods[gi][1], mods[gi][2]
            if i % 2 == 0:
                (h,) = _norm_mod(grp.x, norm1_g[i][None], sc1, sh1, grp.t, (F32,))
                y_in, s_out, last, v_first[gi] = _rwkv_layer(grp, h, shift0[gi][j], s0[gi][j], v_first[gi], j, P,
                                                             chunks[gi])
                wkv_new[gi].append(_slab_to_state(s_out, nheads))
                shift_new[gi].append(last)
                grp.x = _mm(y_in[None], a_w_o, wlead=(j,), res=grp.x, gate=gt1, rows_per_seq=grp.t)[0]
            else:
                (hb,) = _norm_mod(grp.x, norm1_g[i][None], sc1, sh1, grp.t, (BF16,))
                qkv = _mm(hb[None], b_w_qkv, wlead=(j,), bias=b_b_qkv[j][None, None, :])[0]
                qkv3 = qkv.reshape(grp.n, grp.t, -1)
                if gi == 0:
                    att = _swa_prompt(qkv3, b_sinks[j], bias_p, nheads)
                    k_new[gi].append(qkv3[:, -n_rows:, d:d + kvw])
                    v_new[gi].append(qkv3[:, -n_rows:, d + kvw:])
                else:
                    att = _swa_sample(qkv3, ck, cv, j, b_sinks[j], bias_sc, bias_sn, nheads)
                    k_new[gi].append(jnp.concatenate([ck[j], qkv3[:, :, d:d + kvw]], axis=1)[:, -n_rows:])
                    v_new[gi].append(jnp.concatenate([cv[j], qkv3[:, :, d + kvw:]], axis=1)[:, -n_rows:])
                grp.x = _mm(att.reshape(1, grp.n * grp.t, d), b_w_o, wlead=(j,), bias=b_b_o[j][None, None, :],
                            res=grp.x, gate=gt1, rows_per_seq=grp.t)[0]
        hbs = []
        for gi, grp in enumerate(groups):
            sh2, sc2 = mods[gi][3], mods[gi][4]
            hbs.append(_norm_mod(grp.x, norm2_g[i][None], sc2, sh2, grp.t, (BF16,))[0])
        y_all = _moe(jnp.concatenate(hbs, axis=0), i, P)
        lo = 0
        for gi, grp in enumerate(groups):
            m = grp.n * grp.t
            grp.x = _gated_add(grp.x, y_all[lo:lo + m], mods[gi][5], grp.t)
            lo += m

    outs = []
    for gi, grp in enumerate(groups):
        y = _final_norm(grp.x, final_g[None]).reshape(grp.n, grp.t, d)
        kv_shape = (len(k_new[gi]), grp.n, n_rows, nheads // GROUP, HEAD)
        outs.append((y, jnp.stack(wkv_new[gi]), jnp.stack(shift_new[gi]),
                     jnp.stack(k_new[gi]).reshape(kv_shape), jnp.stack(v_new[gi]).reshape(kv_shape)))
    return (outs[0][0], outs[1][0]) + outs[0][1:] + outs[1][1:]
```

```python
import functools
import math

import jax
import jax.numpy as jnp
from jax import lax
from jax.experimental import pallas as pl
from jax.experimental.pallas import tpu as pltpu

F32 = jnp.float32
BF16 = jnp.bfloat16

HEAD = 64
WINDOW = 128
GROUP = 8
N_BUCKETS = 32
MAX_DISTANCE = 128
TOP_K = 4
SWIGLU_ALPHA = 1.702
SWIGLU_LIMIT = 7.0
RMS_EPS = 1e-5
LNX_EPS = 64e-5
NEG = -1e30
QUAD = 4 * HEAD
EXPERT_TILE = 256
VMEM_LIMIT = 56 * 1024 * 1024


def _cparams(n_axes):
    return pltpu.CompilerParams(dimension_semantics=("arbitrary",) * n_axes,
                                vmem_limit_bytes=VMEM_LIMIT)


def _tile(n, pref):
    if n <= pref:
        return n
    t = pref
    while n % t:
        t //= 2
    return t


def _dot(a, b):
    return jnp.dot(a.astype(BF16), b.astype(BF16), preferred_element_type=F32)


def _dot_nt(a, b):
    return lax.dot_general(a.astype(BF16), b.astype(BF16), (((1,), (1,)), ((), ())),
                           preferred_element_type=F32)


def _dot_tn(a, b):
    return lax.dot_general(a.astype(BF16), b.astype(BF16), (((0,), (0,)), ((), ())),
                           preferred_element_type=F32)


def _split3(x):
    h1 = x.astype(BF16)
    r1 = x - h1.astype(F32)
    h2 = r1.astype(BF16)
    h3 = (r1 - h2.astype(F32)).astype(BF16)
    return h1, h2, h3


_NN = (((1,), (0,)), ((), ()))
_NT = (((1,), (1,)), ((), ()))
_TN = (((0,), (0,)), ((), ()))


def _hl(x):
    hi = x.astype(BF16)
    return hi, (x - hi.astype(F32)).astype(BF16)


def _dot3(a, b, dims):
    f = lambda x, y: lax.dot_general(x, y, dims, preferred_element_type=F32)
    return f(a[0], b[0]) + (f(a[0], b[1]) + f(a[1], b[0]))


def _dot_exact_rhs01(x, sel):
    s = sel.astype(BF16)
    out = None
    for p in _split3(x):
        t = jnp.dot(p, s, preferred_element_type=F32)
        out = t if out is None else out + t
    return out


def _dot_exact_lhs01(sel, x):
    s = sel.astype(BF16)
    out = None
    for p in _split3(x):
        t = jnp.dot(s, p, preferred_element_type=F32)
        out = t if out is None else out + t
    return out


def _mm_kernel(*refs, pre_act, act, has_bias, has_res):
    x_ref, w_ref = refs[0], refs[1]
    k = 2
    b_ref = res_ref = gate_ref = None
    if has_bias:
        b_ref = refs[k]
        k += 1
    if has_res:
        res_ref, gate_ref = refs[k], refs[k + 1]
        k += 2
    o_ref, wbf_ref = refs[k], refs[k + 1]

    @pl.when(pl.program_id(2) == 0)
    def _():
        wbf_ref[...] = w_ref[...].astype(BF16)

    x = x_ref[...]
    if pre_act == "silu":
        x = x.astype(F32)
        x = x * jax.nn.sigmoid(x)
    acc = jnp.dot(x.astype(BF16), wbf_ref[...], preferred_element_type=F32)
    if has_bias:
        acc = acc + b_ref[...]
    if act == "tanh":
        acc = jnp.tanh(acc)
    elif act == "sigmoid":
        acc = jax.nn.sigmoid(acc)
    if has_res:
        acc = res_ref[...] + gate_ref[...] * acc
    o_ref[...] = acc.astype(o_ref.dtype)


def _rowvec_spec(arr, tm, tn, rows_per_seq):
    if arr.ndim == 3:
        return pl.BlockSpec((None, 1, tn), lambda g, j, i: ((i * tm) // rows_per_seq, 0, j))
    return pl.BlockSpec((tm, tn), lambda g, j, i: (i, j))


def _mm(x, w, wlead=(), bias=None, pre_act=None, act=None, res=None, gate=None,
        rows_per_seq=None, out_dtype=F32, tm=1024, tn=512, groups=1, xoff=0, x_shared=False, name="mm"):
    _, m, kdim = x.shape
    n = w.shape[-1]
    tm = _tile(rows_per_seq if (gate is not None and gate.ndim == 3) else m, tm)
    tn = _tile(n, tn)
    nlead = len(wlead)

    def w_map(g, j, i):
        return tuple(g if s == "g" else s for s in wlead) + (0, j)

    in_specs = [
        pl.BlockSpec((None, tm, kdim), (lambda g, j, i: (xoff, i, 0)) if x_shared else (lambda g, j, i: (xoff + g, i, 0))),
        pl.BlockSpec((None,) * nlead + (kdim, tn), w_map),
    ]
    args = [x, w]
    if bias is not None:
        gb = bias.shape[0]
        in_specs.append(pl.BlockSpec((None, 1, tn), (lambda g, j, i: (g, 0, j)) if gb > 1 else (lambda g, j, i: (0, 0, j))))
        args.append(bias)
    if res is not None:
        in_specs.append(pl.BlockSpec((tm, tn), lambda g, j, i: (i, j)))
        in_specs.append(_rowvec_spec(gate, tm, tn, rows_per_seq))
        args += [res, gate]
    return pl.pallas_call(
        functools.partial(_mm_kernel, pre_act=pre_act, act=act, has_bias=bias is not None,
                          has_res=res is not None),
        grid=(groups, n // tn, m // tm),
        in_specs=in_specs,
        out_specs=pl.BlockSpec((None, tm, tn), lambda g, j, i: (g, i, j)),
        out_shape=jax.ShapeDtypeStruct((groups, m, n), out_dtype),
        scratch_shapes=[pltpu.VMEM((kdim, tn), BF16)],
        compiler_params=_cparams(3),
        name=name,
    )(*args)


def _norm_mod_kernel(x_ref, g_ref, sc_ref, sh_ref, *o_refs):
    x = x_ref[...]
    y = x * lax.rsqrt(jnp.mean(x * x, axis=-1, keepdims=True) + RMS_EPS)
    h = (y * g_ref[...]) * (1.0 + sc_ref[...]) + sh_ref[...]
    for o in o_refs:
        o[...] = h.astype(o.dtype)


def _vec_spec(arr, tm, d, rows_per_seq):
    if arr.ndim == 3:
        return pl.BlockSpec((None, 1, d), lambda i: ((i * tm) // rows_per_seq, 0, 0))
    return pl.BlockSpec((tm, d), lambda i: (i, 0))


def _norm_mod(x, g, sc, sh, rows_per_seq, out_dtypes, tm=512):
    m, d = x.shape
    tm = _tile(rows_per_seq if sc.ndim == 3 else m, tm)
    outs = pl.pallas_call(
        _norm_mod_kernel,
        grid=(m // tm,),
        in_specs=[pl.BlockSpec((tm, d), lambda i: (i, 0)),
                  pl.BlockSpec((1, d), lambda i: (0, 0)),
                  _vec_spec(sc, tm, d, rows_per_seq),
                  _vec_spec(sh, tm, d, rows_per_seq)],
        out_specs=[pl.BlockSpec((tm, d), lambda i: (i, 0)) for _ in out_dtypes],
        out_shape=[jax.ShapeDtypeStruct((m, d), dt) for dt in out_dtypes],
        compiler_params=_cparams(1),
        name="norm_mod",
    )(x, g, sc, sh)
    return outs


def _final_norm_kernel(x_ref, g_ref, o_ref):
    x = x_ref[...]
    y = x * lax.rsqrt(jnp.mean(x * x, axis=-1, keepdims=True) + RMS_EPS)
    o_ref[...] = y * g_ref[...]


def _final_norm(x, g, tm=512):
    m, d = x.shape
    tm = _tile(m, tm)
    return pl.pallas_call(
        _final_norm_kernel,
        grid=(m // tm,),
        in_specs=[pl.BlockSpec((tm, d), lambda i: (i, 0)), pl.BlockSpec((1, d), lambda i: (0, 0))],
        out_specs=pl.BlockSpec((tm, d), lambda i: (i, 0)),
        out_shape=jax.ShapeDtypeStruct((m, d), F32),
        compiler_params=_cparams(1),
        name="final_norm",
    )(x, g)


def _mix_kernel(h_ref, hp_ref, mix_ref, o_ref):
    h = h_ref[...]
    dlt = hp_ref[...] - h
    for j in range(6):
        o_ref[j] = (h + dlt * mix_ref[j:j + 1, :]).astype(BF16)


def _mix(h, hp, mix, tm=256):
    m, d = h.shape
    tm = _tile(m, tm)
    return pl.pallas_call(
        _mix_kernel,
        grid=(m // tm,),
        in_specs=[pl.BlockSpec((tm, d), lambda i: (i, 0)),
                  pl.BlockSpec((tm, d), lambda i: (i, 0)),
                  pl.BlockSpec((6, d), lambda i: (0, 0))],
        out_specs=pl.BlockSpec((6, tm, d), lambda i: (0, i, 0)),
        out_shape=jax.ShapeDtypeStruct((6, m, d), BF16),
        compiler_params=_cparams(1),
        name="token_shift_mix",
    )(h, hp, mix)


def _prep_kernel(kin_ref, vin_ref, vf_ref, tw_ref, ta_ref, tv_ref, tg_ref,
                 w2_ref, a2_ref, v2_ref, g2_ref, w0_ref, a0_ref, v0_ref, kk_ref, ka_ref,
                 e_ref, et_ref,
                 lw_ref, k_ref, v_ref, a_ref, b_ref, g_ref, *, has_vres):
    k = kin_ref[...]
    v = vin_ref[...]
    z = -(w0_ref[...] + _dot(tw_ref[...], w2_ref[...]))
    softplus = jnp.maximum(z, 0.0) + jnp.log(1.0 + jnp.exp(-jnp.abs(z)))
    w_log = -softplus - 0.5
    lw_ref[...] = -jnp.exp(w_log)
    iclr = jax.nn.sigmoid(a0_ref[...] + _dot(ta_ref[...], a2_ref[...]))
    if has_vres:
        vg = jax.nn.sigmoid(v0_ref[...] + _dot(tv_ref[...], v2_ref[...]))
        v = v + (vf_ref[...] - v) * vg
    v_ref[...] = v
    g_ref[...] = _dot(tg_ref[...], g2_ref[...])
    kk = k * kk_ref[...]
    ss = _dot_exact_rhs01(kk * kk, e_ref[...])
    inv = lax.rsqrt(jnp.maximum(ss, 1e-24))
    kk = kk * _dot_exact_rhs01(inv, et_ref[...])
    k_ref[...] = k * (1.0 + (iclr - 1.0) * ka_ref[...])
    a_ref[...] = -kk
    b_ref[...] = kk * iclr


def _head_selectors(d):
    nh = d // HEAD
    e = (jnp.arange(d)[:, None] // HEAD == jnp.arange(nh)[None, :]).astype(F32)
    return e, e.T


def _prep(rkv, vf, tw, ta, tv, tg, w2, a2, v2, g2, w0, a0, v0, k_k, k_a, has_vres, tm=128):
    _, m, d = rkv.shape
    tm = _tile(m, tm)
    e, et = _head_selectors(d)
    row = lambda a: pl.BlockSpec((tm, a.shape[-1]), lambda i: (i, 0))
    full = lambda a: pl.BlockSpec(a.shape, lambda i: (0,) * a.ndim)
    vecs = [w0, a0, v0, k_k, k_a]
    outs = pl.pallas_call(
        functools.partial(_prep_kernel, has_vres=has_vres),
        grid=(m // tm,),
        in_specs=[pl.BlockSpec((None, tm, d), lambda i: (1, i, 0)), pl.BlockSpec((None, tm, d), lambda i: (2, i, 0)),
                  row(vf), row(tw), row(ta), row(tv), row(tg),
                  full(w2), full(a2), full(v2), full(g2)] + [full(x) for x in vecs] + [full(e), full(et)],
        out_specs=[pl.BlockSpec((tm, d), lambda i: (i, 0))] * 6,
        out_shape=[jax.ShapeDtypeStruct((m, d), F32)] * 6,
        compiler_params=_cparams(1),
        name="rwkv_prep",
    )(rkv, rkv, vf, tw, ta, tv, tg, w2, a2, v2, g2, *vecs, e, et)
    return outs


def _wkv_kernel(r_ref, lw_ref, k_ref, v_ref, a_ref, b_ref, s0_ref, o_ref, sout_ref,
                s_scr, at_scr, rt_scr, kh_scr, bh_scr, kp_scr, bp_scr, v_scr, oq_scr, dg_scr, *, chunk, nquad):
    c = pl.program_id(1)
    rows = 4 * chunk

    @pl.when(c == 0)
    def _():
        s0 = s0_ref[...]
        lane_head = lax.broadcasted_iota(jnp.int32, (HEAD, QUAD), 1) // HEAD
        for q in range(nquad):
            slab = s0[:, q * QUAD:(q + 1) * QUAD]
            s_scr[q] = jnp.concatenate(
                [jnp.where(lane_head == h, slab, 0.0) for h in range(4)], axis=0)

    lw = lw_ref[...]
    ti = lax.broadcasted_iota(jnp.int32, (chunk, chunk), 0)
    tj = lax.broadcasted_iota(jnp.int32, (chunk, chunk), 1)
    cum = _dot_exact_lhs01(ti >= tj, lw)
    cin = jnp.exp(cum)
    inv = jnp.exp(-cum)
    tail = jnp.exp(cum[chunk - 1:chunk, :] - cum)
    a = a_ref[...]
    b = b_ref[...]
    k = k_ref[...]
    v = v_ref[...]
    staged = ((at_scr, a * jnp.exp(cum - lw)), (rt_scr, r_ref[...] * cin), (kh_scr, k * inv),
              (bh_scr, b * inv), (kp_scr, k * tail), (bp_scr, b * tail), (v_scr, v),
              (dg_scr, cin[chunk - 1:chunk, :]))
    for scr, val in staged:
        for q in range(nquad):
            scr[q] = val[:, q * QUAD:(q + 1) * QUAD]

    ri = lax.broadcasted_iota(jnp.int32, (rows, rows), 0)
    rj = lax.broadcasted_iota(jnp.int32, (rows, rows), 1)
    same = (ri // chunk) == (rj // chunk)
    tril_s = same & ((ri % chunk) > (rj % chunk))
    tril_i = same & ((ri % chunk) >= (rj % chunk))
    eye_r = (ri == rj).astype(F32)
    lane_head = lax.broadcasted_iota(jnp.int32, (chunk, QUAD), 1) // HEAD
    qi = lax.broadcasted_iota(jnp.int32, (QUAD, QUAD), 0)
    qj = lax.broadcasted_iota(jnp.int32, (QUAD, QUAD), 1)

    def bd(x):
        return jnp.concatenate([jnp.where(lane_head == h, x, 0.0) for h in range(4)], axis=0)

    def quad_body(q, carry):
        xr_f = bd(rt_scr[q])
        xa = _hl(bd(at_scr[q]))
        xr = _hl(xr_f)
        yk = _hl(bd(kh_scr[q]))
        yb = _hl(bd(bh_scr[q]))
        vb = _hl(bd(v_scr[q]))
        kp = _hl(bd(kp_scr[q]))
        bp = _hl(bd(bp_scr[q]))
        a_ak = _hl(jnp.where(tril_s, _dot3(xa, yk, _NT), 0.0))
        a_ab = jnp.where(tril_s, _dot3(xa, yb, _NT), 0.0)
        a_rk = _hl(jnp.where(tril_i, _dot3(xr, yk, _NT), 0.0))
        a_rb = _hl(jnp.where(tril_i, _dot3(xr, yb, _NT), 0.0))
        tinv = eye_r + a_ab
        p = a_ab
        n = 2
        while n < chunk:
            pb = p.astype(BF16)
            p = lax.dot_general(pb, pb, _NN, preferred_element_type=F32)
            tinv = tinv + _dot(tinv, p)
            n *= 2
        tinv = _hl(tinv)
        av = _hl(_dot3(a_ak, vb, _NN))
        ap = _hl(_dot3(tinv, xa, _NN))
        u0 = _hl(_dot3(tinv, av, _NN))
        rp = _hl(xr_f + _dot3(a_rb, ap, _NN))
        o0 = _dot3(a_rk, vb, _NN) + _dot3(a_rb, u0, _NN)
        dg = jnp.where(qi == qj, dg_scr[q], 0.0)
        mm = _hl(dg + _dot3(ap, bp, _TN))
        sadd = _dot3(vb, kp, _TN) + _dot3(u0, bp, _TN)
        s = _hl(s_scr[q])
        o = _dot3(rp, s, _NT) + o0
        s_new = _dot3(s, mm, _NN) + sadd
        s_scr[q] = s_new
        oq_scr[q] = o[0:chunk] + o[chunk:2 * chunk] + o[2 * chunk:3 * chunk] + o[3 * chunk:4 * chunk]
        return carry

    lax.fori_loop(0, nquad, quad_body, 0, unroll=2)
    for q in range(nquad):
        o_ref[:, q * QUAD:(q + 1) * QUAD] = oq_scr[q]

    @pl.when(c == pl.num_programs(1) - 1)
    def _():
        for q in range(nquad):
            sq = s_scr[q]
            sout_ref[:, q * QUAD:(q + 1) * QUAD] = (sq[0:HEAD] + sq[HEAD:2 * HEAD]
                                                    + sq[2 * HEAD:3 * HEAD] + sq[3 * HEAD:4 * HEAD])


def _wkv(r, lw, k, v, a, b, s0, chunk):
    bsz, t, d = r.shape
    nquad = d // QUAD
    seq = pl.BlockSpec((None, chunk, d), lambda bi, ci: (bi, ci, 0))
    st = pl.BlockSpec((None, HEAD, d), lambda bi, ci: (bi, 0, 0))
    cd = pltpu.VMEM((nquad, chunk, QUAD), F32)
    return pl.pallas_call(
        functools.partial(_wkv_kernel, chunk=chunk, nquad=nquad),
        grid=(bsz, t // chunk),
        in_specs=[seq] * 6 + [st],
        out_specs=[seq, st],
        out_shape=[jax.ShapeDtypeStruct((bsz, t, d), F32), jax.ShapeDtypeStruct((bsz, HEAD, d), F32)],
        scratch_shapes=[pltpu.VMEM((nquad, QUAD, QUAD), F32), cd, cd, cd, cd, cd, cd, cd, cd,
                        pltpu.VMEM((nquad, 1, QUAD), F32)],
        compiler_params=_cparams(2),
        name="wkv_scan",
    )(r, lw, k, v, a, b, s0)


def _post_kernel(o_ref, rkv_ref, k_ref, v_ref, g_ref, lw_ref, lb_ref, rk_ref, e_ref, et_ref, y_ref):
    o = o_ref[...]
    e = e_ref[...]
    et = et_ref[...]
    inv_n = 1.0 / HEAD
    mu = _dot_exact_rhs01(_dot_exact_rhs01(o, e) * inv_n, et)
    dlt = o - mu
    var = _dot_exact_rhs01(dlt * dlt, e) * inv_n
    rstd = _dot_exact_rhs01(lax.rsqrt(var + LNX_EPS), et)
    on = dlt * rstd * lw_ref[...] + lb_ref[...]
    rks = _dot_exact_rhs01(rkv_ref[...] * k_ref[...] * rk_ref[...], e)
    bonus = _dot_exact_rhs01(rks, et) * v_ref[...]
    y_ref[...] = ((on + bonus) * g_ref[...]).astype(BF16)


def _post(o, rkv, k, v, g, lnx_w, lnx_b, r_k, tm=128):
    m, d = o.shape
    tm = _tile(m, tm)
    e, et = _head_selectors(d)
    row = pl.BlockSpec((tm, d), lambda i: (i, 0))
    full = lambda a: pl.BlockSpec(a.shape, lambda i: (0,) * a.ndim)
    return pl.pallas_call(
        _post_kernel,
        grid=(m // tm,),
        in_specs=[row, pl.BlockSpec((None, tm, d), lambda i: (0, i, 0)), row, row, row,
                  full(lnx_w), full(lnx_b), full(r_k), full(e), full(et)],
        out_specs=row,
        out_shape=jax.ShapeDtypeStruct((m, d), BF16),
        compiler_params=_cparams(1),
        name="rwkv_post",
    )(o, rkv, k, v, g, lnx_w, lnx_b, r_k, e, et)


def _t5_bucket(dist):
    max_exact = N_BUCKETS // 2
    n = jnp.maximum(dist, 0)
    nf = jnp.maximum(n, max_exact).astype(F32)
    large = max_exact + (jnp.log(nf / max_exact) / math.log(MAX_DISTANCE / max_exact)
                         * (N_BUCKETS - max_exact)).astype(jnp.int32)
    return jnp.where(n < max_exact, n, jnp.minimum(large, N_BUCKETS - 1))


def _masked_bias(dist, table):
    bias = jnp.transpose(table[_t5_bucket(dist)], (2, 0, 1)).astype(F32)
    valid = (dist >= 0) & (dist < WINDOW)
    return jnp.where(valid[None], bias, NEG)


def _swa_prompt_kernel(sink_ref, q_ref, kp_ref, kc_ref, vp_ref, vc_ref, bias_ref, o_ref, *, nheads, scale):
    first = pl.program_id(1) == 0
    col = lax.broadcasted_iota(jnp.int32, (WINDOW, 2 * WINDOW), 1)
    hide_prev = first & (col < WINDOW)
    for g in range(nheads // GROUP):
        ks = slice(g * HEAD, (g + 1) * HEAD)
        kw = jnp.concatenate([kp_ref[:, ks], kc_ref[:, ks]], axis=0).astype(BF16)
        vw = jnp.concatenate([vp_ref[:, ks], vc_ref[:, ks]], axis=0).astype(BF16)
        for hh in range(GROUP):
            h = g * GROUP + hh
            qh = q_ref[:, h * HEAD:(h + 1) * HEAD]
            s = _dot_nt(qh, kw) * scale + bias_ref[h]
            s = jnp.where(hide_prev, NEG, s)
            sk = sink_ref[h]
            m = jnp.maximum(jnp.max(s, axis=-1, keepdims=True), sk)
            p = jnp.exp(s - m)
            p = p / (jnp.sum(p, axis=-1, keepdims=True) + jnp.exp(sk - m))
            o_ref[:, h * HEAD:(h + 1) * HEAD] = _dot(p, vw).astype(o_ref.dtype)


def _swa_prompt(qkv, sinks, bias, nheads):
    bsz, t, _ = qkv.shape
    d = nheads * HEAD
    kvw = (nheads // GROUP) * HEAD
    nb = t // WINDOW
    kblk = d // kvw
    qspec = pl.BlockSpec((None, WINDOW, d), lambda b, n, s: (b, n, 0))
    kprev = pl.BlockSpec((None, WINDOW, kvw), lambda b, n, s: (b, jnp.maximum(n - 1, 0), kblk))
    kcur = pl.BlockSpec((None, WINDOW, kvw), lambda b, n, s: (b, n, kblk))
    vprev = pl.BlockSpec((None, WINDOW, kvw), lambda b, n, s: (b, jnp.maximum(n - 1, 0), kblk + 1))
    vcur = pl.BlockSpec((None, WINDOW, kvw), lambda b, n, s: (b, n, kblk + 1))
    bspec = pl.BlockSpec(bias.shape, lambda b, n, s: (0, 0, 0))
    return pl.pallas_call(
        functools.partial(_swa_prompt_kernel, nheads=nheads, scale=HEAD ** -0.5),
        grid_spec=pltpu.PrefetchScalarGridSpec(
            num_scalar_prefetch=1,
            grid=(bsz, nb),
            in_specs=[qspec, kprev, kcur, vprev, vcur, bspec],
            out_specs=pl.BlockSpec((None, WINDOW, d), lambda b, n, s: (b, n, 0)),
        ),
        out_shape=jax.ShapeDtypeStruct((bsz, t, d), BF16),
        compiler_params=_cparams(2),
        name="swa_prompt",
    )(sinks, qkv, qkv, qkv, qkv, qkv, bias)


def _swa_sample_kernel(sink_ref, q_ref, kn_ref, vn_ref, ck_ref, cv_ref, bc_ref, bn_ref, o_ref, *, nheads, scale):
    for g in range(nheads // GROUP):
        ks = slice(g * HEAD, (g + 1) * HEAD)
        kc = ck_ref[:, ks].astype(BF16)
        vc = cv_ref[:, ks].astype(BF16)
        kn = kn_ref[:, ks].astype(BF16)
        vn = vn_ref[:, ks].astype(BF16)
        for hh in range(GROUP):
            h = g * GROUP + hh
            qh = q_ref[:, h * HEAD:(h + 1) * HEAD]
            sc = _dot_nt(qh, kc) * scale + bc_ref[h]
            sn = _dot_nt(qh, kn) * scale + bn_ref[h]
            sk = sink_ref[h]
            m = jnp.maximum(jnp.maximum(jnp.max(sc, axis=-1, keepdims=True),
                                        jnp.max(sn, axis=-1, keepdims=True)), sk)
            pc = jnp.exp(sc - m)
            pn = jnp.exp(sn - m)
            den = jnp.sum(pc, axis=-1, keepdims=True) + jnp.sum(pn, axis=-1, keepdims=True) + jnp.exp(sk - m)
            o = _dot(pc / den, vc) + _dot(pn / den, vn)
            o_ref[:, h * HEAD:(h + 1) * HEAD] = o.astype(o_ref.dtype)


def _swa_sample(qkv, ck, cv, lyr, sinks, bias_c, bias_n, nheads):
    bsz, l, _ = qkv.shape
    d = nheads * HEAD
    kvw = (nheads // GROUP) * HEAD
    kblk = d // kvw
    nrows = ck.shape[2]
    full = lambda a: pl.BlockSpec(a.shape, lambda b, s: (0,) * a.ndim)
    return pl.pallas_call(
        functools.partial(_swa_sample_kernel, nheads=nheads, scale=HEAD ** -0.5),
        grid_spec=pltpu.PrefetchScalarGridSpec(
            num_scalar_prefetch=1,
            grid=(bsz,),
            in_specs=[pl.BlockSpec((None, l, d), lambda b, s: (b, 0, 0)),
                      pl.BlockSpec((None, l, kvw), lambda b, s: (b, 0, kblk)),
                      pl.BlockSpec((None, l, kvw), lambda b, s: (b, 0, kblk + 1)),
                      pl.BlockSpec((None, None, nrows, kvw), lambda b, s: (lyr, b, 0, 0)),
                      pl.BlockSpec((None, None, nrows, kvw), lambda b, s: (lyr, b, 0, 0)),
                      full(bias_c), full(bias_n)],
            out_specs=pl.BlockSpec((None, l, d), lambda b, s: (b, 0, 0)),
        ),
        out_shape=jax.ShapeDtypeStruct((bsz, l, d), BF16),
        compiler_params=_cparams(1),
        name="swa_sample",
    )(sinks, qkv, qkv, qkv, ck, cv, bias_c, bias_n)


def _router_kernel(h_ref, w_ref, b_ref, idx_ref, gate_ref, rank_ref, cnt_ref, run_scr, *, n_exp):
    i = pl.program_id(0)
    tm = h_ref.shape[0]

    @pl.when(i == 0)
    def _():
        run_scr[...] = jnp.zeros_like(run_scr)

    logits = _dot(h_ref[...], w_ref[...]) + b_ref[...]
    lane = lax.broadcasted_iota(jnp.int32, (tm, n_exp), 1)
    out_lane = lax.broadcasted_iota(jnp.int32, (tm, 128), 1)
    ri = lax.broadcasted_iota(jnp.int32, (tm, tm), 0)
    rj = lax.broadcasted_iota(jnp.int32, (tm, tm), 1)
    work = logits
    vals, sels = [], []
    chosen = jnp.zeros((tm, n_exp), F32)
    idx_out = jnp.zeros((tm, 128), jnp.int32)
    for kk in range(TOP_K):
        mx = jnp.max(work, axis=-1, keepdims=True)
        idx = jnp.min(jnp.where(work == mx, lane, n_exp), axis=-1, keepdims=True)
        sel = lane == idx
        vals.append(mx)
        sels.append(sel)
        chosen = chosen + sel.astype(F32)
        idx_out = jnp.where(out_lane == kk, idx, idx_out)
        work = jnp.where(sel, -jnp.inf, work)
    es = [jnp.exp(v - vals[0]) for v in vals]
    den = es[0] + es[1] + es[2] + es[3]
    before = jnp.dot((ri > rj).astype(BF16), chosen.astype(BF16), preferred_element_type=F32)
    pos = before + run_scr[...]
    gate_out = jnp.zeros((tm, 128), F32)
    rank_out = jnp.zeros((tm, 128), jnp.int32)
    for kk in range(TOP_K):
        gate_out = jnp.where(out_lane == kk, es[kk] / den, gate_out)
        rk = jnp.sum(jnp.where(sels[kk], pos, 0.0), axis=-1, keepdims=True).astype(jnp.int32)
        rank_out = jnp.where(out_lane == kk, rk, rank_out)
    run_scr[...] = run_scr[...] + jnp.sum(chosen, axis=0, keepdims=True)
    idx_ref[...] = idx_out
    gate_ref[...] = gate_out
    rank_ref[...] = rank_out
    cnt_ref[...] = run_scr[...]


def _router(hb, w_router, b_router, lyr, tm=256):
    t, d = hb.shape
    n_exp = w_router.shape[-1]
    tm = _tile(t, tm)
    wide = pl.BlockSpec((tm, 128), lambda i: (i, 0))
    return pl.pallas_call(
        functools.partial(_router_kernel, n_exp=n_exp),
        grid=(t // tm,),
        in_specs=[pl.BlockSpec((tm, d), lambda i: (i, 0)),
                  pl.BlockSpec((None, d, n_exp), lambda i: (lyr, 0, 0)),
                  pl.BlockSpec((None, 1, n_exp), lambda i: (lyr, 0, 0))],
        out_specs=[wide, wide, wide, pl.BlockSpec((1, n_exp), lambda i: (0, 0))],
        out_shape=[jax.ShapeDtypeStruct((t, 128), jnp.int32), jax.ShapeDtypeStruct((t, 128), F32),
                   jax.ShapeDtypeStruct((t, 128), jnp.int32), jax.ShapeDtypeStruct((1, n_exp), F32)],
        scratch_shapes=[pltpu.VMEM((1, n_exp), F32)],
        compiler_params=_cparams(1),
        name="moe_router",
    )(hb, w_router, b_router)


def _expert_up_kernel(te_ref, nu_ref, x_ref, wg_ref, wu_ref, bg_ref, bu_ref, o_ref, wg_bf, wu_bf):
    i = pl.program_id(1)
    e = te_ref[i]
    prev = te_ref[jnp.maximum(i - 1, 0)]

    @pl.when((i == 0) | (e != prev))
    def _():
        wg_bf[...] = wg_ref[...].astype(BF16)
        wu_bf[...] = wu_ref[...].astype(BF16)

    @pl.when(i < nu_ref[0])
    def _():
        x = x_ref[...].astype(BF16)
        glu = jnp.minimum(jnp.dot(x, wg_bf[...], preferred_element_type=F32) + bg_ref[...], SWIGLU_LIMIT)
        lin = jnp.clip(jnp.dot(x, wu_bf[...], preferred_element_type=F32) + bu_ref[...],
                       -SWIGLU_LIMIT, SWIGLU_LIMIT)
        o_ref[...] = (glu * jax.nn.sigmoid(SWIGLU_ALPHA * glu) * (lin + 1.0)).astype(o_ref.dtype)


def _expert_up(tile_e, n_used, xs, w_gate, w_up, b_gate, b_up, lyr, tf=1024):
    n_slots, d = xs.shape
    f = w_gate.shape[-1]
    tf = _tile(f, tf)
    n_tiles = n_slots // EXPERT_TILE
    wspec = pl.BlockSpec((None, None, d, tf), lambda j, i, te, nu: (lyr, te[i], 0, j))
    bspec = pl.BlockSpec((None, None, 1, tf), lambda j, i, te, nu: (lyr, te[i], 0, j))
    return pl.pallas_call(
        _expert_up_kernel,
        grid_spec=pltpu.PrefetchScalarGridSpec(
            num_scalar_prefetch=2,
            grid=(f // tf, n_tiles),
            in_specs=[pl.BlockSpec((EXPERT_TILE, d), lambda j, i, te, nu: (i, 0)), wspec, wspec, bspec, bspec],
            out_specs=pl.BlockSpec((EXPERT_TILE, tf), lambda j, i, te, nu: (i, j)),
            scratch_shapes=[pltpu.VMEM((d, tf), BF16), pltpu.VMEM((d, tf), BF16)],
        ),
        out_shape=jax.ShapeDtypeStruct((n_slots, f), BF16),
        compiler_params=_cparams(2),
        name="moe_expert_up",
    )(tile_e, n_used, xs, w_gate, w_up, b_gate, b_up)


def _expert_down_kernel(te_ref, nu_ref, h_ref, wd_ref, bd_ref, o_ref, wd_bf):
    i = pl.program_id(1)
    e = te_ref[i]
    prev = te_ref[jnp.maximum(i - 1, 0)]

    @pl.when((i == 0) | (e != prev))
    def _():
        wd_bf[...] = wd_ref[...].astype(BF16)

    @pl.when(i < nu_ref[0])
    def _():
        o_ref[...] = jnp.dot(h_ref[...], wd_bf[...], preferred_element_type=F32) + bd_ref[...]


def _expert_down(tile_e, n_used, hs, w_down, b_down, lyr, tn=2048):
    n_slots, f = hs.shape
    d = w_down.shape[-1]
    tn = _tile(d, tn)
    n_tiles = n_slots // EXPERT_TILE
    return pl.pallas_call(
        _expert_down_kernel,
        grid_spec=pltpu.PrefetchScalarGridSpec(
            num_scalar_prefetch=2,
            grid=(d // tn, n_tiles),
            in_specs=[pl.BlockSpec((EXPERT_TILE, f), lambda j, i, te, nu: (i, 0)),
                      pl.BlockSpec((None, None, f, tn), lambda j, i, te, nu: (lyr, te[i], 0, j)),
                      pl.BlockSpec((None, None, 1, tn), lambda j, i, te, nu: (lyr, te[i], 0, j))],
            out_specs=pl.BlockSpec((EXPERT_TILE, tn), lambda j, i, te, nu: (i, j)),
            scratch_shapes=[pltpu.VMEM((f, tn), BF16)],
        ),
        out_shape=jax.ShapeDtypeStruct((n_slots, d), F32),
        compiler_params=_cparams(2),
        name="moe_expert_down",
    )(tile_e, n_used, hs, w_down, b_down)


def _moe(h, lyr, P):
    t, d = h.shape
    n_exp = P["m_w_router"].shape[-1]
    idx, gate, rank, cnt = _router(h, P["m_w_router"], P["m_b_router"].reshape(-1, 1, n_exp), lyr)
    counts = cnt[0].astype(jnp.int32)
    padded = (counts + EXPERT_TILE - 1) // EXPERT_TILE * EXPERT_TILE
    pend = jnp.cumsum(padded)
    slot = (pend - padded)[idx[:, :TOP_K]] + rank[:, :TOP_K]
    n_tiles = -(-(t * TOP_K) // EXPERT_TILE) + n_exp
    n_slots = n_tiles * EXPERT_TILE
    tok = jnp.broadcast_to(jnp.arange(t, dtype=jnp.int32)[:, None], (t, TOP_K))
    slot_tok = jnp.full((n_slots,), t, jnp.int32).at[slot.reshape(-1)].set(tok.reshape(-1))
    tile_start = jnp.arange(n_tiles, dtype=jnp.int32) * EXPERT_TILE
    tile_e = jnp.minimum(jnp.sum((pend[None, :] <= tile_start[:, None]).astype(jnp.int32), axis=1), n_exp - 1)
    n_used = (pend[-1:] // EXPERT_TILE).astype(jnp.int32)
    xs = jnp.concatenate([h, jnp.zeros((1, d), h.dtype)], axis=0)[slot_tok]
    f = P["m_w_gate"].shape[-1]
    hs = _expert_up(tile_e, n_used, xs, P["m_w_gate"], P["m_w_up"],
                    P["m_b_gate"].reshape(-1, n_exp, 1, f), P["m_b_up"].reshape(-1, n_exp, 1, f), lyr)
    ys = _expert_down(tile_e, n_used, hs, P["m_w_down"], P["m_b_down"].reshape(-1, n_exp, 1, d), lyr)
    return [ys[slot[:, kk]] for kk in range(TOP_K)], gate


def _combine_kernel(x_ref, y0_ref, y1_ref, y2_ref, y3_ref, w_ref, g_ref, o_ref):
    w = w_ref[...]
    y = (y0_ref[...] * w[:, 0:1] + y1_ref[...] * w[:, 1:2]) + (y2_ref[...] * w[:, 2:3] + y3_ref[...] * w[:, 3:4])
    o_ref[...] = x_ref[...] + g_ref[...] * y


def _combine(x, parts, weights, row0, gate, rows_per_seq, tm=512):
    m, d = x.shape
    tm = _tile(rows_per_seq if gate.ndim == 3 else m, tm)
    assert row0 % tm == 0
    off = row0 // tm
    row = pl.BlockSpec((tm, d), lambda i: (i, 0))
    shifted = pl.BlockSpec((tm, d), lambda i: (i + off, 0))
    return pl.pallas_call(
        _combine_kernel,
        grid=(m // tm,),
        in_specs=[row] + [shifted] * TOP_K + [pl.BlockSpec((tm, 128), lambda i: (i + off, 0)),
                                              _vec_spec(gate, tm, d, rows_per_seq)],
        out_specs=row,
        out_shape=jax.ShapeDtypeStruct((m, d), F32),
        compiler_params=_cparams(1),
        name="moe_combine",
    )(x, *parts, weights, gate)


class _Group:
    def __init__(self, x, per_row_mod):
        self.n, self.t, self.d = x.shape
        self.x = x.reshape(self.n * self.t, self.d)
        self.per_row = per_row_mod

    def vec(self, v):
        if self.per_row:
            return jnp.repeat(v, self.t, axis=0)
        return v[:, None, :]


def _rwkv_layer(grp, h, shift0, s0, v_first, j, P, chunk):
    m, d = h.shape
    h3 = h.reshape(grp.n, grp.t, d)
    hp = jnp.concatenate([shift0[:, None, :], h3[:, :-1]], axis=1).reshape(m, d)
    xm = _mix(h, hp, P["a_mix"][j])
    rkv = _mm(xm, P["a_w_rkv"], wlead=(j, "g"), groups=3)
    tw = _mm(xm, P["a_w1"], wlead=(j,), xoff=3, act="tanh", out_dtype=BF16)[0]
    ta = _mm(xm, P["a_a1"], wlead=(j,), xoff=4, out_dtype=BF16)[0]
    tg = _mm(xm, P["a_g1"], wlead=(j,), xoff=5, act="sigmoid", out_dtype=BF16)[0]
    has_vres = j > 0
    if has_vres:
        tv = _mm(xm, P["a_v1"], wlead=(j - 1,), xoff=2, out_dtype=BF16)[0]
        v2, v0, vf = P["a_v2"][j - 1], P["a_v0"][j - 1][None], v_first
    else:
        tv = jnp.zeros((m, 8), BF16)
        v2, v0, vf = jnp.zeros((8, d), F32), jnp.zeros((1, d), F32), jnp.zeros((m, 8), F32)
    lw, k, v, a, b, g = _prep(rkv, vf, tw, ta, tv, tg, P["a_w2"][j], P["a_a2"][j], v2, P["a_g2"][j],
                              P["a_w0"][j][None], P["a_a0"][j][None], v0,
                              P["a_k_k"][j][None], P["a_k_a"][j][None], has_vres)
    if not has_vres:
        v_first = rkv[2]
    t_pad = -(-grp.t // chunk) * chunk

    def seq(z):
        z = z.reshape(grp.n, grp.t, d)
        if t_pad != grp.t:
            z = jnp.pad(z, ((0, 0), (0, t_pad - grp.t), (0, 0)))
        return z

    o, s_out = _wkv(seq(rkv[0]), seq(lw), seq(k), seq(v), seq(a), seq(b), s0, chunk)
    o = o[:, :grp.t].reshape(m, d)
    y_in = _post(o, rkv, k, v, g, P["a_lnx_w"][j][None], P["a_lnx_b"][j][None],
                 P["a_r_k"][j].reshape(1, d))
    return y_in, s_out, h3[:, -1], v_first


def _state_to_slab(s):
    bsz, nh, hv, hk = s.shape
    return jnp.transpose(s, (0, 2, 1, 3)).reshape(bsz, hv, nh * hk)


def _slab_to_state(s, nh):
    bsz, hv, _ = s.shape
    return jnp.transpose(s.reshape(bsz, hv, nh, HEAD), (0, 2, 1, 3))


def kernel(x_prompt, x_sample, c_prompt, c_sample, state_wkv, state_shift, cache_k, cache_v, w_ada, b_ada, norm1_g, norm2_g, final_g, rel_bias_table, a_mix, a_w_rkv, a_w0, a_w1, a_w2, a_a0, a_a1, a_a2, a_v0, a_v1, a_v2, a_g1, a_g2, a_k_k, a_k_a, a_r_k, a_lnx_w, a_lnx_b, a_w_o, b_w_qkv, b_b_qkv, b_sinks, b_w_o, b_b_o, m_w_router, m_b_router, m_w_gate, m_b_gate, m_w_up, m_b_up, m_w_down, m_b_down):
    P = dict(a_mix=a_mix, a_w_rkv=a_w_rkv, a_w0=a_w0, a_w1=a_w1, a_w2=a_w2, a_a0=a_a0, a_a1=a_a1,
             a_a2=a_a2, a_v0=a_v0, a_v1=a_v1, a_v2=a_v2, a_g1=a_g1, a_g2=a_g2, a_k_k=a_k_k,
             a_k_a=a_k_a, a_r_k=a_r_k, a_lnx_w=a_lnx_w, a_lnx_b=a_lnx_b,
             m_w_router=m_w_router, m_b_router=m_b_router, m_w_gate=m_w_gate, m_b_gate=m_b_gate,
             m_w_up=m_w_up, m_b_up=m_b_up, m_w_down=m_w_down, m_b_down=m_b_down)
    depth = w_ada.shape[0]
    d = x_prompt.shape[-1]
    nheads = d // HEAD
    n_p, n_s = x_prompt.shape[0], x_sample.shape[0]
    n_rows = cache_k.shape[2]
    kvw = (nheads // GROUP) * HEAD

    gp = _Group(x_prompt, per_row_mod=False)
    gs = _Group(x_sample, per_row_mod=True)
    groups = (gp, gs)

    n_c = n_p + n_s
    c_all = jnp.concatenate([c_prompt, c_sample], axis=0)
    c_all = jnp.pad(c_all, ((0, -n_c % 8), (0, 0)))[None]
    mod = _mm(c_all, w_ada, wlead=("g",), bias=b_ada[:, None, :], pre_act="silu", groups=depth, x_shared=True,
              tn=1024)

    qi = jnp.arange(WINDOW)[:, None]
    kj = jnp.arange(2 * WINDOW)[None, :]
    bias_p = _masked_bias(qi + WINDOW - kj, rel_bias_table)
    l_s = gs.t
    dist_s = (n_rows + jnp.arange(l_s)[:, None]) - jnp.arange(n_rows + l_s)[None, :]
    bias_s = _masked_bias(dist_s, rel_bias_table)
    bias_sc, bias_sn = bias_s[:, :, :n_rows], bias_s[:, :, n_rows:]

    s0 = {0: jnp.zeros((state_wkv.shape[0], n_p, HEAD, d), F32),
          1: jax.vmap(_state_to_slab)(state_wkv)}
    shift0 = {0: jnp.zeros((state_shift.shape[0], n_p, d), F32), 1: state_shift}
    ck = cache_k.reshape(cache_k.shape[0], n_s, n_rows, kvw)
    cv = cache_v.reshape(cache_v.shape[0], n_s, n_rows, kvw)
    chunks = {0: min(64, gp.t), 1: 16}

    v_first = {0: None, 1: None}
    wkv_new = {0: [], 1: []}
    shift_new = {0: [], 1: []}
    k_new = {0: [], 1: []}
    v_new = {0: [], 1: []}

    for i in range(depth):
        j = i // 2
        mods = []
        for gi, grp in enumerate(groups):
            lo = 0 if gi == 0 else n_p
            mg = mod[i, lo:lo + grp.n]
            mods.append([grp.vec(z) for z in jnp.split(mg, 6, axis=-1)])
        for gi, grp in enumerate(groups):
            sh1, sc1, gt1 = mods[gi][0], mods[gi][1], mods[gi][2]
            if i % 2 == 0:
                (h,) = _norm_mod(grp.x, norm1_g[i][None], sc1, sh1, grp.t, (F32,))
                y_in, s_out, last, v_first[gi] = _rwkv_layer(grp, h, shift0[gi][j], s0[gi][j], v_first[gi], j, P,
                                                             chunks[gi])
                wkv_new[gi].append(_slab_to_state(s_out, nheads))
                shift_new[gi].append(last)
                grp.x = _mm(y_in[None], a_w_o, wlead=(j,), res=grp.x, gate=gt1, rows_per_seq=grp.t)[0]
            else:
                (hb,) = _norm_mod(grp.x, norm1_g[i][None], sc1, sh1, grp.t, (BF16,))
                qkv = _mm(hb[None], b_w_qkv, wlead=(j,), bias=b_b_qkv[j][None, None, :])[0]
                qkv3 = qkv.reshape(grp.n, grp.t, -1)
                if gi == 0:
                    att = _swa_prompt(qkv3, b_sinks[j], bias_p, nheads)
                    k_new[gi].append(qkv3[:, -n_rows:, d:d + kvw])
                    v_new[gi].append(qkv3[:, -n_rows:, d + kvw:])
                else:
                    att = _swa_sample(qkv3, ck, cv, j, b_sinks[j], bias_sc, bias_sn, nheads)
                    k_new[gi].append(jnp.concatenate([ck[j], qkv3[:, :, d:d + kvw]], axis=1)[:, -n_rows:])
                    v_new[gi].append(jnp.concatenate([cv[j], qkv3[:, :, d + kvw:]], axis=1)[:, -n_rows:])
                grp.x = _mm(att.reshape(1, grp.n * grp.t, d), b_w_o, wlead=(j,), bias=b_b_o[j][None, None, :],
                            res=grp.x, gate=gt1, rows_per_seq=grp.t)[0]
        hbs = []
        for gi, grp in enumerate(groups):
            sh2, sc2 = mods[gi][3], mods[gi][4]
            hbs.append(_norm_mod(grp.x, norm2_g[i][None], sc2, sh2, grp.t, (F32,))[0])
        parts, weights = _moe(jnp.concatenate(hbs, axis=0), i, P)
        lo = 0
        for gi, grp in enumerate(groups):
            grp.x = _combine(grp.x, parts, weights, lo, mods[gi][5], grp.t)
            lo += grp.n * grp.t

    outs = []
    for gi, grp in enumerate(groups):
        y = _final_norm(grp.x, final_g[None]).reshape(grp.n, grp.t, d)
        kv_shape = (len(k_new[gi]), grp.n, n_rows, nheads // GROUP, HEAD)
        outs.append((y, jnp.stack(wkv_new[gi]), jnp.stack(shift_new[gi]),
                     jnp.stack(k_new[gi]).reshape(kv_shape), jnp.stack(v_new[gi]).reshape(kv_shape)))
    return (outs[0][0], outs[1][0]) + outs[0][1:] + outs[1][1:]
```

```python
import functools
import math

import jax
import jax.numpy as jnp
from jax import lax
from jax.experimental import pallas as pl
from jax.experimental.pallas import tpu as pltpu

F32 = jnp.float32
BF16 = jnp.bfloat16

HEAD = 64
WINDOW = 128
GROUP = 8
N_BUCKETS = 32
MAX_DISTANCE = 128
TOP_K = 4
SWIGLU_ALPHA = 1.702
SWIGLU_LIMIT = 7.0
RMS_EPS = 1e-5
LNX_EPS = 64e-5
NEG = -1e30
QUAD = 4 * HEAD
EXPERT_TILE = 256
VMEM_LIMIT = 56 * 1024 * 1024


def _cparams(n_axes):
    return pltpu.CompilerParams(dimension_semantics=("arbitrary",) * n_axes,
                                vmem_limit_bytes=VMEM_LIMIT)


def _tile(n, pref):
    if n <= pref:
        return n
    t = pref
    while n % t:
        t //= 2
    return t


def _dot(a, b):
    return jnp.dot(a.astype(BF16), b.astype(BF16), preferred_element_type=F32)


def _dot_nt(a, b):
    return lax.dot_general(a.astype(BF16), b.astype(BF16), (((1,), (1,)), ((), ())),
                           preferred_element_type=F32)


def _dot_tn(a, b):
    return lax.dot_general(a.astype(BF16), b.astype(BF16), (((0,), (0,)), ((), ())),
                           preferred_element_type=F32)


def _split3(x):
    h1 = x.astype(BF16)
    r1 = x - h1.astype(F32)
    h2 = r1.astype(BF16)
    h3 = (r1 - h2.astype(F32)).astype(BF16)
    return h1, h2, h3


_NN = (((1,), (0,)), ((), ()))
_NT = (((1,), (1,)), ((), ()))
_TN = (((0,), (0,)), ((), ()))


def _hl(x):
    hi = x.astype(BF16)
    return hi, (x - hi.astype(F32)).astype(BF16)


def _dot3(a, b, dims):
    f = lambda x, y: lax.dot_general(x, y, dims, preferred_element_type=F32)
    return f(a[0], b[0]) + (f(a[0], b[1]) + f(a[1], b[0]))


def _dot_exact_rhs01(x, sel):
    s = sel.astype(BF16)
    out = None
    for p in _split3(x):
        t = jnp.dot(p, s, preferred_element_type=F32)
        out = t if out is None else out + t
    return out


def _dot_exact_lhs01(sel, x):
    s = sel.astype(BF16)
    out = None
    for p in _split3(x):
        t = jnp.dot(s, p, preferred_element_type=F32)
        out = t if out is None else out + t
    return out


def _mm_kernel(*refs, pre_act, act, has_bias, has_res):
    x_ref, w_ref = refs[0], refs[1]
    k = 2
    b_ref = res_ref = gate_ref = None
    if has_bias:
        b_ref = refs[k]
        k += 1
    if has_res:
        res_ref, gate_ref = refs[k], refs[k + 1]
        k += 2
    o_ref, wbf_ref = refs[k], refs[k + 1]

    @pl.when(pl.program_id(2) == 0)
    def _():
        wbf_ref[...] = w_ref[...].astype(BF16)

    x = x_ref[...]
    if pre_act == "silu":
        x = x.astype(F32)
        x = x * jax.nn.sigmoid(x)
    acc = jnp.dot(x.astype(BF16), wbf_ref[...], preferred_element_type=F32)
    if has_bias:
        acc = acc + b_ref[...]
    if act == "tanh":
        acc = jnp.tanh(acc)
    elif act == "sigmoid":
        acc = jax.nn.sigmoid(acc)
    if has_res:
        acc = res_ref[...] + gate_ref[...] * acc
    o_ref[...] = acc.astype(o_ref.dtype)


def _rowvec_spec(arr, tm, tn, rows_per_seq):
    if arr.ndim == 3:
        return pl.BlockSpec((None, 1, tn), lambda g, j, i: ((i * tm) // rows_per_seq, 0, j))
    return pl.BlockSpec((tm, tn), lambda g, j, i: (i, j))


def _mm(x, w, wlead=(), bias=None, pre_act=None, act=None, res=None, gate=None,
        rows_per_seq=None, out_dtype=F32, tm=1024, tn=512, groups=1, xoff=0, x_shared=False, name="mm"):
    _, m, kdim = x.shape
    n = w.shape[-1]
    tm = _tile(rows_per_seq if (gate is not None and gate.ndim == 3) else m, tm)
    tn = _tile(n, tn)
    nlead = len(wlead)

    def w_map(g, j, i):
        return tuple(g if s == "g" else s for s in wlead) + (0, j)

    in_specs = [
        pl.BlockSpec((None, tm, kdim), (lambda g, j, i: (xoff, i, 0)) if x_shared else (lambda g, j, i: (xoff + g, i, 0))),
        pl.BlockSpec((None,) * nlead + (kdim, tn), w_map),
    ]
    args = [x, w]
    if bias is not None:
        gb = bias.shape[0]
        in_specs.append(pl.BlockSpec((None, 1, tn), (lambda g, j, i: (g, 0, j)) if gb > 1 else (lambda g, j, i: (0, 0, j))))
        args.append(bias)
    if res is not None:
        in_specs.append(pl.BlockSpec((tm, tn), lambda g, j, i: (i, j)))
        in_specs.append(_rowvec_spec(gate, tm, tn, rows_per_seq))
        args += [res, gate]
    return pl.pallas_call(
        functools.partial(_mm_kernel, pre_act=pre_act, act=act, has_bias=bias is not None,
                          has_res=res is not None),
        grid=(groups, n // tn, m // tm),
        in_specs=in_specs,
        out_specs=pl.BlockSpec((None, tm, tn), lambda g, j, i: (g, i, j)),
        out_shape=jax.ShapeDtypeStruct((groups, m, n), out_dtype),
        scratch_shapes=[pltpu.VMEM((kdim, tn), BF16)],
        compiler_params=_cparams(3),
        name=name,
    )(*args)


def _norm_mod_kernel(x_ref, g_ref, sc_ref, sh_ref, *o_refs):
    x = x_ref[...]
    y = x * lax.rsqrt(jnp.mean(x * x, axis=-1, keepdims=True) + RMS_EPS)
    h = (y * g_ref[...]) * (1.0 + sc_ref[...]) + sh_ref[...]
    for o in o_refs:
        o[...] = h.astype(o.dtype)


def _vec_spec(arr, tm, d, rows_per_seq):
    if arr.ndim == 3:
        return pl.BlockSpec((None, 1, d), lambda i: ((i * tm) // rows_per_seq, 0, 0))
    return pl.BlockSpec((tm, d), lambda i: (i, 0))


def _norm_mod(x, g, sc, sh, rows_per_seq, out_dtypes, tm=512):
    m, d = x.shape
    tm = _tile(rows_per_seq if sc.ndim == 3 else m, tm)
    outs = pl.pallas_call(
        _norm_mod_kernel,
        grid=(m // tm,),
        in_specs=[pl.BlockSpec((tm, d), lambda i: (i, 0)),
                  pl.BlockSpec((1, d), lambda i: (0, 0)),
                  _vec_spec(sc, tm, d, rows_per_seq),
                  _vec_spec(sh, tm, d, rows_per_seq)],
        out_specs=[pl.BlockSpec((tm, d), lambda i: (i, 0)) for _ in out_dtypes],
        out_shape=[jax.ShapeDtypeStruct((m, d), dt) for dt in out_dtypes],
        compiler_params=_cparams(1),
        name="norm_mod",
    )(x, g, sc, sh)
    return outs


def _final_norm_kernel(x_ref, g_ref, o_ref):
    x = x_ref[...]
    y = x * lax.rsqrt(jnp.mean(x * x, axis=-1, keepdims=True) + RMS_EPS)
    o_ref[...] = y * g_ref[...]


def _final_norm(x, g, tm=512):
    m, d = x.shape
    tm = _tile(m, tm)
    return pl.pallas_call(
        _final_norm_kernel,
        grid=(m // tm,),
        in_specs=[pl.BlockSpec((tm, d), lambda i: (i, 0)), pl.BlockSpec((1, d), lambda i: (0, 0))],
        out_specs=pl.BlockSpec((tm, d), lambda i: (i, 0)),
        out_shape=jax.ShapeDtypeStruct((m, d), F32),
        compiler_params=_cparams(1),
        name="final_norm",
    )(x, g)


def _mix_kernel(h_ref, hp_ref, mix_ref, o_ref):
    h = h_ref[...]
    dlt = hp_ref[...] - h
    for j in range(6):
        o_ref[j] = (h + dlt * mix_ref[j:j + 1, :]).astype(BF16)


def _mix(h, hp, mix, tm=256):
    m, d = h.shape
    tm = _tile(m, tm)
    return pl.pallas_call(
        _mix_kernel,
        grid=(m // tm,),
        in_specs=[pl.BlockSpec((tm, d), lambda i: (i, 0)),
                  pl.BlockSpec((tm, d), lambda i: (i, 0)),
                  pl.BlockSpec((6, d), lambda i: (0, 0))],
        out_specs=pl.BlockSpec((6, tm, d), lambda i: (0, i, 0)),
        out_shape=jax.ShapeDtypeStruct((6, m, d), BF16),
        compiler_params=_cparams(1),
        name="token_shift_mix",
    )(h, hp, mix)


def _prep_kernel(kin_ref, vin_ref, vf_ref, tw_ref, ta_ref, tv_ref, tg_ref,
                 w2_ref, a2_ref, v2_ref, g2_ref, w0_ref, a0_ref, v0_ref, kk_ref, ka_ref,
                 e_ref, et_ref,
                 lw_ref, k_ref, v_ref, a_ref, b_ref, g_ref, *, has_vres):
    k = kin_ref[...]
    v = vin_ref[...]
    z = -(w0_ref[...] + _dot(tw_ref[...], w2_ref[...]))
    softplus = jnp.maximum(z, 0.0) + jnp.log(1.0 + jnp.exp(-jnp.abs(z)))
    w_log = -softplus - 0.5
    lw_ref[...] = -jnp.exp(w_log)
    iclr = jax.nn.sigmoid(a0_ref[...] + _dot(ta_ref[...], a2_ref[...]))
    if has_vres:
        vg = jax.nn.sigmoid(v0_ref[...] + _dot(tv_ref[...], v2_ref[...]))
        v = v + (vf_ref[...] - v) * vg
    v_ref[...] = v
    g_ref[...] = _dot(tg_ref[...], g2_ref[...])
    kk = k * kk_ref[...]
    ss = _dot_exact_rhs01(kk * kk, e_ref[...])
    inv = lax.rsqrt(jnp.maximum(ss, 1e-24))
    kk = kk * _dot_exact_rhs01(inv, et_ref[...])
    k_ref[...] = k * (1.0 + (iclr - 1.0) * ka_ref[...])
    a_ref[...] = -kk
    b_ref[...] = kk * iclr


def _head_selectors(d):
    nh = d // HEAD
    e = (jnp.arange(d)[:, None] // HEAD == jnp.arange(nh)[None, :]).astype(F32)
    return e, e.T


def _prep(rkv, vf, tw, ta, tv, tg, w2, a2, v2, g2, w0, a0, v0, k_k, k_a, has_vres, tm=128):
    _, m, d = rkv.shape
    tm = _tile(m, tm)
    e, et = _head_selectors(d)
    row = lambda a: pl.BlockSpec((tm, a.shape[-1]), lambda i: (i, 0))
    full = lambda a: pl.BlockSpec(a.shape, lambda i: (0,) * a.ndim)
    vecs = [w0, a0, v0, k_k, k_a]
    outs = pl.pallas_call(
        functools.partial(_prep_kernel, has_vres=has_vres),
        grid=(m // tm,),
        in_specs=[pl.BlockSpec((None, tm, d), lambda i: (1, i, 0)), pl.BlockSpec((None, tm, d), lambda i: (2, i, 0)),
                  row(vf), row(tw), row(ta), row(tv), row(tg),
                  full(w2), full(a2), full(v2), full(g2)] + [full(x) for x in vecs] + [full(e), full(et)],
        out_specs=[pl.BlockSpec((tm, d), lambda i: (i, 0))] * 6,
        out_shape=[jax.ShapeDtypeStruct((m, d), F32)] * 6,
        compiler_params=_cparams(1),
        name="rwkv_prep",
    )(rkv, rkv, vf, tw, ta, tv, tg, w2, a2, v2, g2, *vecs, e, et)
    return outs


def _wkv_kernel(r_ref, lw_ref, k_ref, v_ref, a_ref, b_ref, s0_ref, o_ref, sout_ref,
                s_scr, at_scr, rt_scr, kh_scr, bh_scr, kp_scr, bp_scr, v_scr, oq_scr, dg_scr, *, chunk, nquad):
    c = pl.program_id(1)
    rows = 4 * chunk

    @pl.when(c == 0)
    def _():
        s0 = s0_ref[...]
        lane_head = lax.broadcasted_iota(jnp.int32, (HEAD, QUAD), 1) // HEAD
        for q in range(nquad):
            slab = s0[:, q * QUAD:(q + 1) * QUAD]
            s_scr[q] = jnp.concatenate(
                [jnp.where(lane_head == h, slab, 0.0) for h in range(4)], axis=0)

    lw = lw_ref[...]
    ti = lax.broadcasted_iota(jnp.int32, (chunk, chunk), 0)
    tj = lax.broadcasted_iota(jnp.int32, (chunk, chunk), 1)
    cum = _dot_exact_lhs01(ti >= tj, lw)
    cin = jnp.exp(cum)
    inv = jnp.exp(-cum)
    tail = jnp.exp(cum[chunk - 1:chunk, :] - cum)
    a = a_ref[...]
    b = b_ref[...]
    k = k_ref[...]
    v = v_ref[...]
    staged = ((at_scr, a * jnp.exp(cum - lw)), (rt_scr, r_ref[...] * cin), (kh_scr, k * inv),
              (bh_scr, b * inv), (kp_scr, k * tail), (bp_scr, b * tail), (v_scr, v),
              (dg_scr, cin[chunk - 1:chunk, :]))
    for scr, val in staged:
        for q in range(nquad):
            scr[q] = val[:, q * QUAD:(q + 1) * QUAD]

    ri = lax.broadcasted_iota(jnp.int32, (rows, rows), 0)
    rj = lax.broadcasted_iota(jnp.int32, (rows, rows), 1)
    same = (ri // chunk) == (rj // chunk)
    tril_s = same & ((ri % chunk) > (rj % chunk))
    tril_i = same & ((ri % chunk) >= (rj % chunk))
    eye_r = (ri == rj).astype(F32)
    lane_head = lax.broadcasted_iota(jnp.int32, (chunk, QUAD), 1) // HEAD
    qi = lax.broadcasted_iota(jnp.int32, (QUAD, QUAD), 0)
    qj = lax.broadcasted_iota(jnp.int32, (QUAD, QUAD), 1)

    def bd(x):
        return jnp.concatenate([jnp.where(lane_head == h, x, 0.0) for h in range(4)], axis=0)

    def quad_body(q, carry):
        xr_f = bd(rt_scr[q])
        xa = _hl(bd(at_scr[q]))
        xr = _hl(xr_f)
        yk = _hl(bd(kh_scr[q]))
        yb = _hl(bd(bh_scr[q]))
        vb = _hl(bd(v_scr[q]))
        kp = _hl(bd(kp_scr[q]))
        bp = _hl(bd(bp_scr[q]))
        a_ak = _hl(jnp.where(tril_s, _dot3(xa, yk, _NT), 0.0))
        a_ab = jnp.where(tril_s, _dot3(xa, yb, _NT), 0.0)
        a_rk = _hl(jnp.where(tril_i, _dot3(xr, yk, _NT), 0.0))
        a_rb = _hl(jnp.where(tril_i, _dot3(xr, yb, _NT), 0.0))
        tinv = eye_r + a_ab
        p = a_ab
        n = 2
        while n < chunk:
            pb = p.astype(BF16)
            p = lax.dot_general(pb, pb, _NN, preferred_element_type=F32)
            tinv = tinv + _dot(tinv, p)
            n *= 2
        tinv = _hl(tinv)
        av = _hl(_dot3(a_ak, vb, _NN))
        ap = _hl(_dot3(tinv, xa, _NN))
        u0 = _hl(_dot3(tinv, av, _NN))
        rp = _hl(xr_f + _dot3(a_rb, ap, _NN))
        o0 = _dot3(a_rk, vb, _NN) + _dot3(a_rb, u0, _NN)
        dg = jnp.where(qi == qj, dg_scr[q], 0.0)
        mm = _hl(dg + _dot3(ap, bp, _TN))
        sadd = _dot3(vb, kp, _TN) + _dot3(u0, bp, _TN)
        s = _hl(s_scr[q])
        o = _dot3(rp, s, _NT) + o0
        s_new = _dot3(s, mm, _NN) + sadd
        s_scr[q] = s_new
        oq_scr[q] = o[0:chunk] + o[chunk:2 * chunk] + o[2 * chunk:3 * chunk] + o[3 * chunk:4 * chunk]
        return carry

    lax.fori_loop(0, nquad, quad_body, 0, unroll=2)
    for q in range(nquad):
        o_ref[:, q * QUAD:(q + 1) * QUAD] = oq_scr[q]

    @pl.when(c == pl.num_programs(1) - 1)
    def _():
        for q in range(nquad):
            sq = s_scr[q]
            sout_ref[:, q * QUAD:(q + 1) * QUAD] = (sq[0:HEAD] + sq[HEAD:2 * HEAD]
                                                    + sq[2 * HEAD:3 * HEAD] + sq[3 * HEAD:4 * HEAD])


def _wkv(r, lw, k, v, a, b, s0, chunk):
    bsz, t, d = r.shape
    nquad = d // QUAD
    seq = pl.BlockSpec((None, chunk, d), lambda bi, ci: (bi, ci, 0))
    st = pl.BlockSpec((None, HEAD, d), lambda bi, ci: (bi, 0, 0))
    cd = pltpu.VMEM((nquad, chunk, QUAD), F32)
    return pl.pallas_call(
        functools.partial(_wkv_kernel, chunk=chunk, nquad=nquad),
        grid=(bsz, t // chunk),
        in_specs=[seq] * 6 + [st],
        out_specs=[seq, st],
        out_shape=[jax.ShapeDtypeStruct((bsz, t, d), F32), jax.ShapeDtypeStruct((bsz, HEAD, d), F32)],
        scratch_shapes=[pltpu.VMEM((nquad, QUAD, QUAD), F32), cd, cd, cd, cd, cd, cd, cd, cd,
                        pltpu.VMEM((nquad, 1, QUAD), F32)],
        compiler_params=_cparams(2),
        name="wkv_scan",
    )(r, lw, k, v, a, b, s0)


def _post_kernel(o_ref, rkv_ref, k_ref, v_ref, g_ref, lw_ref, lb_ref, rk_ref, e_ref, et_ref, y_ref):
    o = o_ref[...]
    e = e_ref[...]
    et = et_ref[...]
    inv_n = 1.0 / HEAD
    mu = _dot_exact_rhs01(_dot_exact_rhs01(o, e) * inv_n, et)
    dlt = o - mu
    var = _dot_exact_rhs01(dlt * dlt, e) * inv_n
    rstd = _dot_exact_rhs01(lax.rsqrt(var + LNX_EPS), et)
    on = dlt * rstd * lw_ref[...] + lb_ref[...]
    rks = _dot_exact_rhs01(rkv_ref[...] * k_ref[...] * rk_ref[...], e)
    bonus = _dot_exact_rhs01(rks, et) * v_ref[...]
    y_ref[...] = ((on + bonus) * g_ref[...]).astype(BF16)


def _post(o, rkv, k, v, g, lnx_w, lnx_b, r_k, tm=128):
    m, d = o.shape
    tm = _tile(m, tm)
    e, et = _head_selectors(d)
    row = pl.BlockSpec((tm, d), lambda i: (i, 0))
    full = lambda a: pl.BlockSpec(a.shape, lambda i: (0,) * a.ndim)
    return pl.pallas_call(
        _post_kernel,
        grid=(m // tm,),
        in_specs=[row, pl.BlockSpec((None, tm, d), lambda i: (0, i, 0)), row, row, row,
                  full(lnx_w), full(lnx_b), full(r_k), full(e), full(et)],
        out_specs=row,
        out_shape=jax.ShapeDtypeStruct((m, d), BF16),
        compiler_params=_cparams(1),
        name="rwkv_post",
    )(o, rkv, k, v, g, lnx_w, lnx_b, r_k, e, et)


def _t5_bucket(dist):
    max_exact = N_BUCKETS // 2
    n = jnp.maximum(dist, 0)
    nf = jnp.maximum(n, max_exact).astype(F32)
    large = max_exact + (jnp.log(nf / max_exact) / math.log(MAX_DISTANCE / max_exact)
                         * (N_BUCKETS - max_exact)).astype(jnp.int32)
    return jnp.where(n < max_exact, n, jnp.minimum(large, N_BUCKETS - 1))


def _masked_bias(dist, table):
    bias = jnp.transpose(table[_t5_bucket(dist)], (2, 0, 1)).astype(F32)
    valid = (dist >= 0) & (dist < WINDOW)
    return jnp.where(valid[None], bias, NEG)


def _swa_prompt_kernel(sink_ref, q_ref, kp_ref, kc_ref, vp_ref, vc_ref, bias_ref, o_ref, *, nheads, scale):
    first = pl.program_id(1) == 0
    col = lax.broadcasted_iota(jnp.int32, (WINDOW, 2 * WINDOW), 1)
    hide_prev = first & (col < WINDOW)
    for g in range(nheads // GROUP):
        ks = slice(g * HEAD, (g + 1) * HEAD)
        kw = jnp.concatenate([kp_ref[:, ks], kc_ref[:, ks]], axis=0).astype(BF16)
        vw = jnp.concatenate([vp_ref[:, ks], vc_ref[:, ks]], axis=0).astype(BF16)
        for hh in range(GROUP):
            h = g * GROUP + hh
            qh = q_ref[:, h * HEAD:(h + 1) * HEAD]
            s = _dot_nt(qh, kw) * scale + bias_ref[h]
            s = jnp.where(hide_prev, NEG, s)
            sk = sink_ref[h]
            m = jnp.maximum(jnp.max(s, axis=-1, keepdims=True), sk)
            p = jnp.exp(s - m)
            p = p / (jnp.sum(p, axis=-1, keepdims=True) + jnp.exp(sk - m))
            o_ref[:, h * HEAD:(h + 1) * HEAD] = _dot(p, vw).astype(o_ref.dtype)


def _swa_prompt(qkv, sinks, bias, nheads):
    bsz, t, _ = qkv.shape
    d = nheads * HEAD
    kvw = (nheads // GROUP) * HEAD
    nb = t // WINDOW
    kblk = d // kvw
    qspec = pl.BlockSpec((None, WINDOW, d), lambda b, n, s: (b, n, 0))
    kprev = pl.BlockSpec((None, WINDOW, kvw), lambda b, n, s: (b, jnp.maximum(n - 1, 0), kblk))
    kcur = pl.BlockSpec((None, WINDOW, kvw), lambda b, n, s: (b, n, kblk))
    vprev = pl.BlockSpec((None, WINDOW, kvw), lambda b, n, s: (b, jnp.maximum(n - 1, 0), kblk + 1))
    vcur = pl.BlockSpec((None, WINDOW, kvw), lambda b, n, s: (b, n, kblk + 1))
    bspec = pl.BlockSpec(bias.shape, lambda b, n, s: (0, 0, 0))
    return pl.pallas_call(
        functools.partial(_swa_prompt_kernel, nheads=nheads, scale=HEAD ** -0.5),
        grid_spec=pltpu.PrefetchScalarGridSpec(
            num_scalar_prefetch=1,
            grid=(bsz, nb),
            in_specs=[qspec, kprev, kcur, vprev, vcur, bspec],
            out_specs=pl.BlockSpec((None, WINDOW, d), lambda b, n, s: (b, n, 0)),
        ),
        out_shape=jax.ShapeDtypeStruct((bsz, t, d), BF16),
        compiler_params=_cparams(2),
        name="swa_prompt",
    )(sinks, qkv, qkv, qkv, qkv, qkv, bias)


def _swa_sample_kernel(sink_ref, q_ref, kn_ref, vn_ref, ck_ref, cv_ref, bc_ref, bn_ref, o_ref, *, nheads, scale):
    for g in range(nheads // GROUP):
        ks = slice(g * HEAD, (g + 1) * HEAD)
        kc = ck_ref[:, ks].astype(BF16)
        vc = cv_ref[:, ks].astype(BF16)
        kn = kn_ref[:, ks].astype(BF16)
        vn = vn_ref[:, ks].astype(BF16)
        for hh in range(GROUP):
            h = g * GROUP + hh
            qh = q_ref[:, h * HEAD:(h + 1) * HEAD]
            sc = _dot_nt(qh, kc) * scale + bc_ref[h]
            sn = _dot_nt(qh, kn) * scale + bn_ref[h]
            sk = sink_ref[h]
            m = jnp.maximum(jnp.maximum(jnp.max(sc, axis=-1, keepdims=True),
                                        jnp.max(sn, axis=-1, keepdims=True)), sk)
            pc = jnp.exp(sc - m)
            pn = jnp.exp(sn - m)
            den = jnp.sum(pc, axis=-1, keepdims=True) + jnp.sum(pn, axis=-1, keepdims=True) + jnp.exp(sk - m)
            o = _dot(pc / den, vc) + _dot(pn / den, vn)
            o_ref[:, h * HEAD:(h + 1) * HEAD] = o.astype(o_ref.dtype)


def _swa_sample(qkv, ck, cv, lyr, sinks, bias_c, bias_n, nheads):
    bsz, l, _ = qkv.shape
    d = nheads * HEAD
    kvw = (nheads // GROUP) * HEAD
    kblk = d // kvw
    nrows = ck.shape[2]
    full = lambda a: pl.BlockSpec(a.shape, lambda b, s: (0,) * a.ndim)
    return pl.pallas_call(
        functools.partial(_swa_sample_kernel, nheads=nheads, scale=HEAD ** -0.5),
        grid_spec=pltpu.PrefetchScalarGridSpec(
            num_scalar_prefetch=1,
            grid=(bsz,),
            in_specs=[pl.BlockSpec((None, l, d), lambda b, s: (b, 0, 0)),
                      pl.BlockSpec((None, l, kvw), lambda b, s: (b, 0, kblk)),
                      pl.BlockSpec((None, l, kvw), lambda b, s: (b, 0, kblk + 1)),
                      pl.BlockSpec((None, None, nrows, kvw), lambda b, s: (lyr, b, 0, 0)),
                      pl.BlockSpec((None, None, nrows, kvw), lambda b, s: (lyr, b, 0, 0)),
                      full(bias_c), full(bias_n)],
            out_specs=pl.BlockSpec((None, l, d), lambda b, s: (b, 0, 0)),
        ),
        out_shape=jax.ShapeDtypeStruct((bsz, l, d), BF16),
        compiler_params=_cparams(1),
        name="swa_sample",
    )(sinks, qkv, qkv, qkv, ck, cv, bias_c, bias_n)


def _router_kernel(h_ref, w_ref, b_ref, idx_ref, gate_ref, rank_ref, cnt_ref, run_scr, *, n_exp):
    i = pl.program_id(0)
    tm = h_ref.shape[0]

    @pl.when(i == 0)
    def _():
        run_scr[...] = jnp.zeros_like(run_scr)

    logits = _dot(h_ref[...], w_ref[...]) + b_ref[...]
    lane = lax.broadcasted_iota(jnp.int32, (tm, n_exp), 1)
    out_lane = lax.broadcasted_iota(jnp.int32, (tm, 128), 1)
    ri = lax.broadcasted_iota(jnp.int32, (tm, tm), 0)
    rj = lax.broadcasted_iota(jnp.int32, (tm, tm), 1)
    work = logits
    vals, sels = [], []
    chosen = jnp.zeros((tm, n_exp), F32)
    idx_out = jnp.zeros((tm, 128), jnp.int32)
    for kk in range(TOP_K):
        mx = jnp.max(work, axis=-1, keepdims=True)
        idx = jnp.min(jnp.where(work == mx, lane, n_exp), axis=-1, keepdims=True)
        sel = lane == idx
        vals.append(mx)
        sels.append(sel)
        chosen = chosen + sel.astype(F32)
        idx_out = jnp.where(out_lane == kk, idx, idx_out)
        work = jnp.where(sel, -jnp.inf, work)
    es = [jnp.exp(v - vals[0]) for v in vals]
    den = es[0] + es[1] + es[2] + es[3]
    before = jnp.dot((ri > rj).astype(BF16), chosen.astype(BF16), preferred_element_type=F32)
    pos = before + run_scr[...]
    gate_out = jnp.zeros((tm, 128), F32)
    rank_out = jnp.zeros((tm, 128), jnp.int32)
    for kk in range(TOP_K):
        gate_out = jnp.where(out_lane == kk, es[kk] / den, gate_out)
        rk = jnp.sum(jnp.where(sels[kk], pos, 0.0), axis=-1, keepdims=True).astype(jnp.int32)
        rank_out = jnp.where(out_lane == kk, rk, rank_out)
    run_scr[...] = run_scr[...] + jnp.sum(chosen, axis=0, keepdims=True)
    idx_ref[...] = idx_out
    gate_ref[...] = gate_out
    rank_ref[...] = rank_out
    cnt_ref[...] = run_scr[...]


def _router(hb, w_router, b_router, lyr, tm=256):
    t, d = hb.shape
    n_exp = w_router.shape[-1]
    tm = _tile(t, tm)
    wide = pl.BlockSpec((tm, 128), lambda i: (i, 0))
    return pl.pallas_call(
        functools.partial(_router_kernel, n_exp=n_exp),
        grid=(t // tm,),
        in_specs=[pl.BlockSpec((tm, d), lambda i: (i, 0)),
                  pl.BlockSpec((None, d, n_exp), lambda i: (lyr, 0, 0)),
                  pl.BlockSpec((None, 1, n_exp), lambda i: (lyr, 0, 0))],
        out_specs=[wide, wide, wide, pl.BlockSpec((1, n_exp), lambda i: (0, 0))],
        out_shape=[jax.ShapeDtypeStruct((t, 128), jnp.int32), jax.ShapeDtypeStruct((t, 128), F32),
                   jax.ShapeDtypeStruct((t, 128), jnp.int32), jax.ShapeDtypeStruct((1, n_exp), F32)],
        scratch_shapes=[pltpu.VMEM((1, n_exp), F32)],
        compiler_params=_cparams(1),
        name="moe_router",
    )(hb, w_router, b_router)


def _weight_copies(w_refs, stage, sems, lyr, e, col, width, slot):
    return [pltpu.make_async_copy(w.at[lyr, e, :, pl.ds(col, width)], stage.at[slot, m], sems.at[slot, m])
            for m, w in enumerate(w_refs)]


def _stream_expert_weights(te_ref, rid_ref, re_ref, nr_ref, w_refs, stage, sems, wbf_refs, lyr, width):
    j = pl.program_id(0)
    i = pl.program_id(1)
    e = te_ref[i]
    first = (i == 0) | (e != te_ref[jnp.maximum(i - 1, 0)])
    n_runs = nr_ref[0]
    run = rid_ref[i]
    g = j * n_runs + run
    slot = g % 2

    @pl.when(first)
    def _():
        col = pl.multiple_of(j * width, width)

        @pl.when(g == 0)
        def _():
            for c in _weight_copies(w_refs, stage, sems, lyr, e, col, width, slot):
                c.start()

        for c in _weight_copies(w_refs, stage, sems, lyr, e, col, width, slot):
            c.wait()
        wrap = run + 1 >= n_runs
        run_n = jnp.where(wrap, 0, run + 1)
        j_n = jnp.where(wrap, j + 1, j)

        @pl.when(j_n < pl.num_programs(0))
        def _():
            col_n = pl.multiple_of(j_n * width, width)
            for c in _weight_copies(w_refs, stage, sems, lyr, re_ref[run_n], col_n, width, 1 - slot):
                c.start()

        for m, wbf in enumerate(wbf_refs):
            wbf[...] = stage[slot, m].astype(BF16)


def _expert_up_kernel(te_ref, nu_ref, rid_ref, re_ref, nr_ref, x_ref, wg_hbm, wu_hbm, bg_ref, bu_ref, o_ref,
                      stage, sems, wg_bf, wu_bf, *, lyr, tf):
    i = pl.program_id(1)
    _stream_expert_weights(te_ref, rid_ref, re_ref, nr_ref, (wg_hbm, wu_hbm), stage, sems, (wg_bf, wu_bf), lyr, tf)

    @pl.when(i < nu_ref[0])
    def _():
        x = x_ref[...].astype(BF16)
        glu = jnp.minimum(jnp.dot(x, wg_bf[...], preferred_element_type=F32) + bg_ref[...], SWIGLU_LIMIT)
        lin = jnp.clip(jnp.dot(x, wu_bf[...], preferred_element_type=F32) + bu_ref[...],
                       -SWIGLU_LIMIT, SWIGLU_LIMIT)
        o_ref[...] = (glu * jax.nn.sigmoid(SWIGLU_ALPHA * glu) * (lin + 1.0)).astype(o_ref.dtype)

    @pl.when(i >= nu_ref[0])
    def _():
        o_ref[...] = jnp.zeros_like(o_ref)


def _expert_up(sched, xs, w_gate, w_up, b_gate, b_up, lyr, tf=1024):
    n_slots, d = xs.shape
    f = w_gate.shape[-1]
    tf = _tile(f, tf)
    n_tiles = n_slots // EXPERT_TILE
    bspec = pl.BlockSpec((None, None, 1, tf), lambda j, i, te, *_: (lyr, te[i], 0, j))
    hbm = pl.BlockSpec(memory_space=pl.ANY)
    return pl.pallas_call(
        functools.partial(_expert_up_kernel, lyr=lyr, tf=tf),
        grid_spec=pltpu.PrefetchScalarGridSpec(
            num_scalar_prefetch=5,
            grid=(f // tf, n_tiles),
            in_specs=[pl.BlockSpec((EXPERT_TILE, d), lambda j, i, *_: (i, 0)), hbm, hbm, bspec, bspec],
            out_specs=pl.BlockSpec((EXPERT_TILE, tf), lambda j, i, *_: (i, j)),
            scratch_shapes=[pltpu.VMEM((2, 2, d, tf), F32), pltpu.SemaphoreType.DMA((2, 2)),
                            pltpu.VMEM((d, tf), BF16), pltpu.VMEM((d, tf), BF16)],
        ),
        out_shape=jax.ShapeDtypeStruct((n_slots, f), BF16),
        compiler_params=_cparams(2),
        name="moe_expert_up",
    )(*sched, xs, w_gate, w_up, b_gate, b_up)


def _expert_down_kernel(te_ref, nu_ref, rid_ref, re_ref, nr_ref, h_ref, wd_hbm, bd_ref, o_ref,
                        stage, sems, wd_bf, *, lyr, tn):
    i = pl.program_id(1)
    _stream_expert_weights(te_ref, rid_ref, re_ref, nr_ref, (wd_hbm,), stage, sems, (wd_bf,), lyr, tn)

    @pl.when(i < nu_ref[0])
    def _():
        o_ref[...] = jnp.dot(h_ref[...], wd_bf[...], preferred_element_type=F32) + bd_ref[...]

    @pl.when(i >= nu_ref[0])
    def _():
        o_ref[...] = jnp.zeros_like(o_ref)


def _expert_down(sched, hs, w_down, b_down, lyr, tn=2048):
    n_slots, f = hs.shape
    d = w_down.shape[-1]
    tn = _tile(d, tn)
    n_tiles = n_slots // EXPERT_TILE
    return pl.pallas_call(
        functools.partial(_expert_down_kernel, lyr=lyr, tn=tn),
        grid_spec=pltpu.PrefetchScalarGridSpec(
            num_scalar_prefetch=5,
            grid=(d // tn, n_tiles),
            in_specs=[pl.BlockSpec((EXPERT_TILE, f), lambda j, i, *_: (i, 0)),
                      pl.BlockSpec(memory_space=pl.ANY),
                      pl.BlockSpec((None, None, 1, tn), lambda j, i, te, *_: (lyr, te[i], 0, j))],
            out_specs=pl.BlockSpec((EXPERT_TILE, tn), lambda j, i, *_: (i, j)),
            scratch_shapes=[pltpu.VMEM((2, 1, f, tn), F32), pltpu.SemaphoreType.DMA((2, 1)),
                            pltpu.VMEM((f, tn), BF16)],
        ),
        out_shape=jax.ShapeDtypeStruct((n_slots, d), F32),
        compiler_params=_cparams(2),
        name="moe_expert_down",
    )(*sched, hs, w_down, b_down)


def _moe(h, lyr, P):
    t, d = h.shape
    n_exp = P["m_w_router"].shape[-1]
    idx, gate, rank, cnt = _router(h, P["m_w_router"], P["m_b_router"].reshape(-1, 1, n_exp), lyr)
    counts = cnt[0].astype(jnp.int32)
    padded = (counts + EXPERT_TILE - 1) // EXPERT_TILE * EXPERT_TILE
    pend = jnp.cumsum(padded)
    slot = (pend - padded)[idx[:, :TOP_K]] + rank[:, :TOP_K]
    n_tiles = -(-(t * TOP_K) // EXPERT_TILE) + n_exp
    n_slots = n_tiles * EXPERT_TILE
    tok = jnp.broadcast_to(jnp.arange(t, dtype=jnp.int32)[:, None], (t, TOP_K))
    slot_tok = jnp.full((n_slots,), t, jnp.int32).at[slot.reshape(-1)].set(tok.reshape(-1))
    tile_start = jnp.arange(n_tiles, dtype=jnp.int32) * EXPERT_TILE
    tile_e = jnp.minimum(jnp.sum((pend[None, :] <= tile_start[:, None]).astype(jnp.int32), axis=1), n_exp - 1)
    n_used = (pend[-1:] // EXPERT_TILE).astype(jnp.int32)
    change = jnp.concatenate([jnp.ones((1,), jnp.int32), (tile_e[1:] != tile_e[:-1]).astype(jnp.int32)])
    run_id = jnp.cumsum(change) - 1
    run_e = jnp.zeros((n_tiles,), jnp.int32).at[run_id].set(tile_e)
    sched = (tile_e, n_used, run_id, run_e, run_id[-1:] + 1)
    xs = jnp.concatenate([h, jnp.zeros((1, d), h.dtype)], axis=0)[slot_tok]
    f = P["m_w_gate"].shape[-1]
    hs = _expert_up(sched, xs, P["m_w_gate"], P["m_w_up"],
                    P["m_b_gate"].reshape(-1, n_exp, 1, f), P["m_b_up"].reshape(-1, n_exp, 1, f), lyr)
    ys = _expert_down(sched, hs, P["m_w_down"], P["m_b_down"].reshape(-1, n_exp, 1, d), lyr)
    return [ys[slot[:, kk]] for kk in range(TOP_K)], gate


def _combine_kernel(x_ref, y0_ref, y1_ref, y2_ref, y3_ref, w_ref, g_ref, o_ref):
    w = w_ref[...]
    y = (y0_ref[...] * w[:, 0:1] + y1_ref[...] * w[:, 1:2]) + (y2_ref[...] * w[:, 2:3] + y3_ref[...] * w[:, 3:4])
    o_ref[...] = x_ref[...] + g_ref[...] * y


def _combine(x, parts, weights, row0, gate, rows_per_seq, tm=512):
    m, d = x.shape
    tm = _tile(rows_per_seq if gate.ndim == 3 else m, tm)
    assert row0 % tm == 0
    off = row0 // tm
    row = pl.BlockSpec((tm, d), lambda i: (i, 0))
    shifted = pl.BlockSpec((tm, d), lambda i: (i + off, 0))
    return pl.pallas_call(
        _combine_kernel,
        grid=(m // tm,),
        in_specs=[row] + [shifted] * TOP_K + [pl.BlockSpec((tm, 128), lambda i: (i + off, 0)),
                                              _vec_spec(gate, tm, d, rows_per_seq)],
        out_specs=row,
        out_shape=jax.ShapeDtypeStruct((m, d), F32),
        compiler_params=_cparams(1),
        name="moe_combine",
    )(x, *parts, weights, gate)


class _Group:
    def __init__(self, x, per_row_mod):
        self.n, self.t, self.d = x.shape
        self.x = x.reshape(self.n * self.t, self.d)
        self.per_row = per_row_mod

    def vec(self, v):
        if self.per_row:
            return jnp.repeat(v, self.t, axis=0)
        return v[:, None, :]


def _rwkv_layer(grp, h, shift0, s0, v_first, j, P, chunk):
    m, d = h.shape
    h3 = h.reshape(grp.n, grp.t, d)
    hp = jnp.concatenate([shift0[:, None, :], h3[:, :-1]], axis=1).reshape(m, d)
    xm = _mix(h, hp, P["a_mix"][j])
    rkv = _mm(xm, P["a_w_rkv"], wlead=(j, "g"), groups=3)
    tw = _mm(xm, P["a_w1"], wlead=(j,), xoff=3, act="tanh", out_dtype=BF16)[0]
    ta = _mm(xm, P["a_a1"], wlead=(j,), xoff=4, out_dtype=BF16)[0]
    tg = _mm(xm, P["a_g1"], wlead=(j,), xoff=5, act="sigmoid", out_dtype=BF16)[0]
    has_vres = j > 0
    if has_vres:
        tv = _mm(xm, P["a_v1"], wlead=(j - 1,), xoff=2, out_dtype=BF16)[0]
        v2, v0, vf = P["a_v2"][j - 1], P["a_v0"][j - 1][None], v_first
    else:
        tv = jnp.zeros((m, 8), BF16)
        v2, v0, vf = jnp.zeros((8, d), F32), jnp.zeros((1, d), F32), jnp.zeros((m, 8), F32)
    lw, k, v, a, b, g = _prep(rkv, vf, tw, ta, tv, tg, P["a_w2"][j], P["a_a2"][j], v2, P["a_g2"][j],
                              P["a_w0"][j][None], P["a_a0"][j][None], v0,
                              P["a_k_k"][j][None], P["a_k_a"][j][None], has_vres)
    if not has_vres:
        v_first = rkv[2]
    t_pad = -(-grp.t // chunk) * chunk

    def seq(z):
        z = z.reshape(grp.n, grp.t, d)
        if t_pad != grp.t:
            z = jnp.pad(z, ((0, 0), (0, t_pad - grp.t), (0, 0)))
        return z

    o, s_out = _wkv(seq(rkv[0]), seq(lw), seq(k), seq(v), seq(a), seq(b), s0, chunk)
    o = o[:, :grp.t].reshape(m, d)
    y_in = _post(o, rkv, k, v, g, P["a_lnx_w"][j][None], P["a_lnx_b"][j][None],
                 P["a_r_k"][j].reshape(1, d))
    return y_in, s_out, h3[:, -1], v_first


def _state_to_slab(s):
    bsz, nh, hv, hk = s.shape
    return jnp.transpose(s, (0, 2, 1, 3)).reshape(bsz, hv, nh * hk)


def _slab_to_state(s, nh):
    bsz, hv, _ = s.shape
    return jnp.transpose(s.reshape(bsz, hv, nh, HEAD), (0, 2, 1, 3))


def kernel(x_prompt, x_sample, c_prompt, c_sample, state_wkv, state_shift, cache_k, cache_v, w_ada, b_ada, norm1_g, norm2_g, final_g, rel_bias_table, a_mix, a_w_rkv, a_w0, a_w1, a_w2, a_a0, a_a1, a_a2, a_v0, a_v1, a_v2, a_g1, a_g2, a_k_k, a_k_a, a_r_k, a_lnx_w, a_lnx_b, a_w_o, b_w_qkv, b_b_qkv, b_sinks, b_w_o, b_b_o, m_w_router, m_b_router, m_w_gate, m_b_gate, m_w_up, m_b_up, m_w_down, m_b_down):
    P = dict(a_mix=a_mix, a_w_rkv=a_w_rkv, a_w0=a_w0, a_w1=a_w1, a_w2=a_w2, a_a0=a_a0, a_a1=a_a1,
             a_a2=a_a2, a_v0=a_v0, a_v1=a_v1, a_v2=a_v2, a_g1=a_g1, a_g2=a_g2, a_k_k=a_k_k,
             a_k_a=a_k_a, a_r_k=a_r_k, a_lnx_w=a_lnx_w, a_lnx_b=a_lnx_b,
             m_w_router=m_w_router, m_b_router=m_b_router, m_w_gate=m_w_gate, m_b_gate=m_b_gate,
             m_w_up=m_w_up, m_b_up=m_b_up, m_w_down=m_w_down, m_b_down=m_b_down)
    depth = w_ada.shape[0]
    d = x_prompt.shape[-1]
    nheads = d // HEAD
    n_p, n_s = x_prompt.shape[0], x_sample.shape[0]
    n_rows = cache_k.shape[2]
    kvw = (nheads // GROUP) * HEAD

    gp = _Group(x_prompt, per_row_mod=False)
    gs = _Group(x_sample, per_row_mod=True)
    groups = (gp, gs)

    n_c = n_p + n_s
    c_all = jnp.concatenate([c_prompt, c_sample], axis=0)
    c_all = jnp.pad(c_all, ((0, -n_c % 8), (0, 0)))[None]
    mod = _mm(c_all, w_ada, wlead=("g",), bias=b_ada[:, None, :], pre_act="silu", groups=depth, x_shared=True,
              tn=1024)

    qi = jnp.arange(WINDOW)[:, None]
    kj = jnp.arange(2 * WINDOW)[None, :]
    bias_p = _masked_bias(qi + WINDOW - kj, rel_bias_table)
    l_s = gs.t
    dist_s = (n_rows + jnp.arange(l_s)[:, None]) - jnp.arange(n_rows + l_s)[None, :]
    bias_s = _masked_bias(dist_s, rel_bias_table)
    bias_sc, bias_sn = bias_s[:, :, :n_rows], bias_s[:, :, n_rows:]

    s0 = {0: jnp.zeros((state_wkv.shape[0], n_p, HEAD, d), F32),
          1: jax.vmap(_state_to_slab)(state_wkv)}
    shift0 = {0: jnp.zeros((state_shift.shape[0], n_p, d), F32), 1: state_shift}
    ck = cache_k.reshape(cache_k.shape[0], n_s, n_rows, kvw)
    cv = cache_v.reshape(cache_v.shape[0], n_s, n_rows, kvw)
    chunks = {0: min(64, gp.t), 1: 16}

    v_first = {0: None, 1: None}
    wkv_new = {0: [], 1: []}
    shift_new = {0: [], 1: []}
    k_new = {0: [], 1: []}
    v_new = {0: [], 1: []}

    for i in range(depth):
        j = i // 2
        mods = []
        for gi, grp in enumerate(groups):
            lo = 0 if gi == 0 else n_p
            mg = mod[i, lo:lo + grp.n]
            mods.append([grp.vec(z) for z in jnp.split(mg, 6, axis=-1)])
        for gi, grp in enumerate(groups):
            sh1, sc1, gt1 = mods[gi][0], mods[gi][1], mods[gi][2]
            if i % 2 == 0:
                (h,) = _norm_mod(grp.x, norm1_g[i][None], sc1, sh1, grp.t, (F32,))
                y_in, s_out, last, v_first[gi] = _rwkv_layer(grp, h, shift0[gi][j], s0[gi][j], v_first[gi], j, P,
                                                             chunks[gi])
                wkv_new[gi].append(_slab_to_state(s_out, nheads))
                shift_new[gi].append(last)
                grp.x = _mm(y_in[None], a_w_o, wlead=(j,), res=grp.x, gate=gt1, rows_per_seq=grp.t)[0]
            else:
                (hb,) = _norm_mod(grp.x, norm1_g[i][None], sc1, sh1, grp.t, (BF16,))
                qkv = _mm(hb[None], b_w_qkv, wlead=(j,), bias=b_b_qkv[j][None, None, :])[0]
                qkv3 = qkv.reshape(grp.n, grp.t, -1)
                if gi == 0:
                    att = _swa_prompt(qkv3, b_sinks[j], bias_p, nheads)
                    k_new[gi].append(qkv3[:, -n_rows:, d:d + kvw])
                    v_new[gi].append(qkv3[:, -n_rows:, d + kvw:])
                else:
                    att = _swa_sample(qkv3, ck, cv, j, b_sinks[j], bias_sc, bias_sn, nheads)
                    k_new[gi].append(jnp.concatenate([ck[j], qkv3[:, :, d:d + kvw]], axis=1)[:, -n_rows:])
                    v_new[gi].append(jnp.concatenate([cv[j], qkv3[:, :, d + kvw:]], axis=1)[:, -n_rows:])
                grp.x = _mm(att.reshape(1, grp.n * grp.t, d), b_w_o, wlead=(j,), bias=b_b_o[j][None, None, :],
                            res=grp.x, gate=gt1, rows_per_seq=grp.t)[0]
        hbs = []
        for gi, grp in enumerate(groups):
            sh2, sc2 = mods[gi][3], mods[gi][4]
            hbs.append(_norm_mod(grp.x, norm2_g[i][None], sc2, sh2, grp.t, (F32,))[0])
        parts, weights = _moe(jnp.concatenate(hbs, axis=0), i, P)
        lo = 0
        for gi, grp in enumerate(groups):
            grp.x = _combine(grp.x, parts, weights, lo, mods[gi][5], grp.t)
            lo += grp.n * grp.t

    outs = []
    for gi, grp in enumerate(groups):
        y = _final_norm(grp.x, final_g[None]).reshape(grp.n, grp.t, d)
        kv_shape = (len(k_new[gi]), grp.n, n_rows, nheads // GROUP, HEAD)
        outs.append((y, jnp.stack(wkv_new[gi]), jnp.stack(shift_new[gi]),
                     jnp.stack(k_new[gi]).reshape(kv_shape), jnp.stack(v_new[gi]).reshape(kv_shape)))
    return (outs[0][0], outs[1][0]) + outs[0][1:] + outs[1][1:]
```

```python
import functools
import math

import jax
import jax.numpy as jnp
from jax import lax
from jax.experimental import pallas as pl
from jax.experimental.pallas import tpu as pltpu

F32 = jnp.float32
BF16 = jnp.bfloat16

HEAD = 64
WINDOW = 128
GROUP = 8
N_BUCKETS = 32
MAX_DISTANCE = 128
TOP_K = 4
SWIGLU_ALPHA = 1.702
SWIGLU_LIMIT = 7.0
RMS_EPS = 1e-5
LNX_EPS = 64e-5
NEG = -1e30
QUAD = 4 * HEAD
EXPERT_TILE = 256
VMEM_LIMIT = 56 * 1024 * 1024


def _cparams(n_axes):
    return pltpu.CompilerParams(dimension_semantics=("arbitrary",) * n_axes,
                                vmem_limit_bytes=VMEM_LIMIT)


def _tile(n, pref):
    if n <= pref:
        return n
    t = pref
    while n % t:
        t //= 2
    return t


def _dot(a, b):
    return jnp.dot(a.astype(BF16), b.astype(BF16), preferred_element_type=F32)


def _dot_nt(a, b):
    return lax.dot_general(a.astype(BF16), b.astype(BF16), (((1,), (1,)), ((), ())),
                           preferred_element_type=F32)


def _dot_tn(a, b):
    return lax.dot_general(a.astype(BF16), b.astype(BF16), (((0,), (0,)), ((), ())),
                           preferred_element_type=F32)


def _split3(x):
    h1 = x.astype(BF16)
    r1 = x - h1.astype(F32)
    h2 = r1.astype(BF16)
    h3 = (r1 - h2.astype(F32)).astype(BF16)
    return h1, h2, h3


_NN = (((1,), (0,)), ((), ()))
_NT = (((1,), (1,)), ((), ()))
_TN = (((0,), (0,)), ((), ()))


def _hl(x):
    hi = x.astype(BF16)
    return hi, (x - hi.astype(F32)).astype(BF16)


def _dot3(a, b, dims):
    f = lambda x, y: lax.dot_general(x, y, dims, preferred_element_type=F32)
    return f(a[0], b[0]) + (f(a[0], b[1]) + f(a[1], b[0]))


def _dot_exact_rhs01(x, sel):
    s = sel.astype(BF16)
    out = None
    for p in _split3(x):
        t = jnp.dot(p, s, preferred_element_type=F32)
        out = t if out is None else out + t
    return out


def _dot_exact_lhs01(sel, x):
    s = sel.astype(BF16)
    out = None
    for p in _split3(x):
        t = jnp.dot(s, p, preferred_element_type=F32)
        out = t if out is None else out + t
    return out


def _mm_kernel(*refs, pre_act, act, has_bias, has_res):
    x_ref, w_ref = refs[0], refs[1]
    k = 2
    b_ref = res_ref = gate_ref = None
    if has_bias:
        b_ref = refs[k]
        k += 1
    if has_res:
        res_ref, gate_ref = refs[k], refs[k + 1]
        k += 2
    o_ref, wbf_ref = refs[k], refs[k + 1]

    @pl.when(pl.program_id(2) == 0)
    def _():
        wbf_ref[...] = w_ref[...].astype(BF16)

    x = x_ref[...]
    if pre_act == "silu":
        x = x.astype(F32)
        x = x * jax.nn.sigmoid(x)
    acc = jnp.dot(x.astype(BF16), wbf_ref[...], preferred_element_type=F32)
    if has_bias:
        acc = acc + b_ref[...]
    if act == "tanh":
        acc = jnp.tanh(acc)
    elif act == "sigmoid":
        acc = jax.nn.sigmoid(acc)
    if has_res:
        acc = res_ref[...] + gate_ref[...] * acc
    o_ref[...] = acc.astype(o_ref.dtype)


def _rowvec_spec(arr, tm, tn, rows_per_seq):
    if arr.ndim == 3:
        return pl.BlockSpec((None, 1, tn), lambda g, j, i: ((i * tm) // rows_per_seq, 0, j))
    return pl.BlockSpec((tm, tn), lambda g, j, i: (i, j))


def _mm(x, w, wlead=(), bias=None, pre_act=None, act=None, res=None, gate=None,
        rows_per_seq=None, out_dtype=F32, tm=1024, tn=512, groups=1, xoff=0, x_shared=False, name="mm"):
    _, m, kdim = x.shape
    n = w.shape[-1]
    tm = _tile(rows_per_seq if (gate is not None and gate.ndim == 3) else m, tm)
    tn = _tile(n, tn)
    nlead = len(wlead)

    def w_map(g, j, i):
        return tuple(g if s == "g" else s for s in wlead) + (0, j)

    in_specs = [
        pl.BlockSpec((None, tm, kdim), (lambda g, j, i: (xoff, i, 0)) if x_shared else (lambda g, j, i: (xoff + g, i, 0))),
        pl.BlockSpec((None,) * nlead + (kdim, tn), w_map),
    ]
    args = [x, w]
    if bias is not None:
        gb = bias.shape[0]
        in_specs.append(pl.BlockSpec((None, 1, tn), (lambda g, j, i: (g, 0, j)) if gb > 1 else (lambda g, j, i: (0, 0, j))))
        args.append(bias)
    if res is not None:
        in_specs.append(pl.BlockSpec((tm, tn), lambda g, j, i: (i, j)))
        in_specs.append(_rowvec_spec(gate, tm, tn, rows_per_seq))
        args += [res, gate]
    return pl.pallas_call(
        functools.partial(_mm_kernel, pre_act=pre_act, act=act, has_bias=bias is not None,
                          has_res=res is not None),
        grid=(groups, n // tn, m // tm),
        in_specs=in_specs,
        out_specs=pl.BlockSpec((None, tm, tn), lambda g, j, i: (g, i, j)),
        out_shape=jax.ShapeDtypeStruct((groups, m, n), out_dtype),
        scratch_shapes=[pltpu.VMEM((kdim, tn), BF16)],
        compiler_params=_cparams(3),
        name=name,
    )(*args)


def _norm_mod_kernel(x_ref, g_ref, sc_ref, sh_ref, *o_refs):
    x = x_ref[...]
    y = x * lax.rsqrt(jnp.mean(x * x, axis=-1, keepdims=True) + RMS_EPS)
    h = (y * g_ref[...]) * (1.0 + sc_ref[...]) + sh_ref[...]
    for o in o_refs:
        o[...] = h.astype(o.dtype)


def _vec_spec(arr, tm, d, rows_per_seq):
    if arr.ndim == 3:
        return pl.BlockSpec((None, 1, d), lambda i: ((i * tm) // rows_per_seq, 0, 0))
    return pl.BlockSpec((tm, d), lambda i: (i, 0))


def _norm_mod(x, g, sc, sh, rows_per_seq, out_dtypes, tm=512):
    m, d = x.shape
    tm = _tile(rows_per_seq if sc.ndim == 3 else m, tm)
    outs = pl.pallas_call(
        _norm_mod_kernel,
        grid=(m // tm,),
        in_specs=[pl.BlockSpec((tm, d), lambda i: (i, 0)),
                  pl.BlockSpec((1, d), lambda i: (0, 0)),
                  _vec_spec(sc, tm, d, rows_per_seq),
                  _vec_spec(sh, tm, d, rows_per_seq)],
        out_specs=[pl.BlockSpec((tm, d), lambda i: (i, 0)) for _ in out_dtypes],
        out_shape=[jax.ShapeDtypeStruct((m, d), dt) for dt in out_dtypes],
        compiler_params=_cparams(1),
        name="norm_mod",
    )(x, g, sc, sh)
    return outs


def _final_norm_kernel(x_ref, g_ref, o_ref):
    x = x_ref[...]
    y = x * lax.rsqrt(jnp.mean(x * x, axis=-1, keepdims=True) + RMS_EPS)
    o_ref[...] = y * g_ref[...]


def _final_norm(x, g, tm=512):
    m, d = x.shape
    tm = _tile(m, tm)
    return pl.pallas_call(
        _final_norm_kernel,
        grid=(m // tm,),
        in_specs=[pl.BlockSpec((tm, d), lambda i: (i, 0)), pl.BlockSpec((1, d), lambda i: (0, 0))],
        out_specs=pl.BlockSpec((tm, d), lambda i: (i, 0)),
        out_shape=jax.ShapeDtypeStruct((m, d), F32),
        compiler_params=_cparams(1),
        name="final_norm",
    )(x, g)


def _mix_kernel(h_ref, hp_ref, mix_ref, o_ref):
    h = h_ref[...]
    dlt = hp_ref[...] - h
    for j in range(6):
        o_ref[j] = (h + dlt * mix_ref[j:j + 1, :]).astype(BF16)


def _mix(h, hp, mix, tm=256):
    m, d = h.shape
    tm = _tile(m, tm)
    return pl.pallas_call(
        _mix_kernel,
        grid=(m // tm,),
        in_specs=[pl.BlockSpec((tm, d), lambda i: (i, 0)),
                  pl.BlockSpec((tm, d), lambda i: (i, 0)),
                  pl.BlockSpec((6, d), lambda i: (0, 0))],
        out_specs=pl.BlockSpec((6, tm, d), lambda i: (0, i, 0)),
        out_shape=jax.ShapeDtypeStruct((6, m, d), BF16),
        compiler_params=_cparams(1),
        name="token_shift_mix",
    )(h, hp, mix)


def _prep_kernel(kin_ref, vin_ref, vf_ref, tw_ref, ta_ref, tv_ref, tg_ref,
                 w2_ref, a2_ref, v2_ref, g2_ref, w0_ref, a0_ref, v0_ref, kk_ref, ka_ref,
                 e_ref, et_ref,
                 lw_ref, k_ref, v_ref, a_ref, b_ref, g_ref, *, has_vres):
    k = kin_ref[...]
    v = vin_ref[...]
    z = -(w0_ref[...] + _dot(tw_ref[...], w2_ref[...]))
    softplus = jnp.maximum(z, 0.0) + jnp.log(1.0 + jnp.exp(-jnp.abs(z)))
    w_log = -softplus - 0.5
    lw_ref[...] = -jnp.exp(w_log)
    iclr = jax.nn.sigmoid(a0_ref[...] + _dot(ta_ref[...], a2_ref[...]))
    if has_vres:
        vg = jax.nn.sigmoid(v0_ref[...] + _dot(tv_ref[...], v2_ref[...]))
        v = v + (vf_ref[...] - v) * vg
    v_ref[...] = v
    g_ref[...] = _dot(tg_ref[...], g2_ref[...])
    kk = k * kk_ref[...]
    ss = _dot_exact_rhs01(kk * kk, e_ref[...])
    inv = lax.rsqrt(jnp.maximum(ss, 1e-24))
    kk = kk * _dot_exact_rhs01(inv, et_ref[...])
    k_ref[...] = k * (1.0 + (iclr - 1.0) * ka_ref[...])
    a_ref[...] = -kk
    b_ref[...] = kk * iclr


def _head_selectors(d):
    nh = d // HEAD
    e = (jnp.arange(d)[:, None] // HEAD == jnp.arange(nh)[None, :]).astype(F32)
    return e, e.T


def _prep(rkv, vf, tw, ta, tv, tg, w2, a2, v2, g2, w0, a0, v0, k_k, k_a, has_vres, tm=128):
    _, m, d = rkv.shape
    tm = _tile(m, tm)
    e, et = _head_selectors(d)
    row = lambda a: pl.BlockSpec((tm, a.shape[-1]), lambda i: (i, 0))
    full = lambda a: pl.BlockSpec(a.shape, lambda i: (0,) * a.ndim)
    vf_spec = pl.BlockSpec((None, tm, d), lambda i: (2, i, 0)) if vf.ndim == 3 else row(vf)
    vecs = [w0, a0, v0, k_k, k_a]
    outs = pl.pallas_call(
        functools.partial(_prep_kernel, has_vres=has_vres),
        grid=(m // tm,),
        in_specs=[pl.BlockSpec((None, tm, d), lambda i: (1, i, 0)), pl.BlockSpec((None, tm, d), lambda i: (2, i, 0)),
                  vf_spec, row(tw), row(ta), row(tv), row(tg),
                  full(w2), full(a2), full(v2), full(g2)] + [full(x) for x in vecs] + [full(e), full(et)],
        out_specs=[pl.BlockSpec((tm, d), lambda i: (i, 0))] * 6,
        out_shape=[jax.ShapeDtypeStruct((m, d), F32)] * 6,
        compiler_params=_cparams(1),
        name="rwkv_prep",
    )(rkv, rkv, vf, tw, ta, tv, tg, w2, a2, v2, g2, *vecs, e, et)
    return outs


def _wkv_kernel(r_ref, lw_ref, k_ref, v_ref, a_ref, b_ref, s0_ref, o_ref, sout_ref,
                s_scr, at_scr, rt_scr, kh_scr, bh_scr, kp_scr, bp_scr, v_scr, oq_scr, dg_scr, *, chunk, nquad):
    c = pl.program_id(1)
    rows = 4 * chunk

    @pl.when(c == 0)
    def _():
        s0 = s0_ref[...]
        lane_head = lax.broadcasted_iota(jnp.int32, (HEAD, QUAD), 1) // HEAD
        for q in range(nquad):
            slab = s0[:, q * QUAD:(q + 1) * QUAD]
            s_scr[q] = jnp.concatenate(
                [jnp.where(lane_head == h, slab, 0.0) for h in range(4)], axis=0)

    lw = lw_ref[...]
    ti = lax.broadcasted_iota(jnp.int32, (chunk, chunk), 0)
    tj = lax.broadcasted_iota(jnp.int32, (chunk, chunk), 1)
    cum = _dot_exact_lhs01(ti >= tj, lw)
    cin = jnp.exp(cum)
    inv = jnp.exp(-cum)
    tail = jnp.exp(cum[chunk - 1:chunk, :] - cum)
    a = a_ref[...]
    b = b_ref[...]
    k = k_ref[...]
    v = v_ref[...]
    staged = ((at_scr, a * jnp.exp(cum - lw)), (rt_scr, r_ref[...] * cin), (kh_scr, k * inv),
              (bh_scr, b * inv), (kp_scr, k * tail), (bp_scr, b * tail), (v_scr, v),
              (dg_scr, cin[chunk - 1:chunk, :]))
    for scr, val in staged:
        for q in range(nquad):
            scr[q] = val[:, q * QUAD:(q + 1) * QUAD]

    ri = lax.broadcasted_iota(jnp.int32, (rows, rows), 0)
    rj = lax.broadcasted_iota(jnp.int32, (rows, rows), 1)
    same = (ri // chunk) == (rj // chunk)
    tril_s = same & ((ri % chunk) > (rj % chunk))
    tril_i = same & ((ri % chunk) >= (rj % chunk))
    eye_r = (ri == rj).astype(F32)
    lane_head = lax.broadcasted_iota(jnp.int32, (chunk, QUAD), 1) // HEAD
    qi = lax.broadcasted_iota(jnp.int32, (QUAD, QUAD), 0)
    qj = lax.broadcasted_iota(jnp.int32, (QUAD, QUAD), 1)

    def bd(x):
        return jnp.concatenate([jnp.where(lane_head == h, x, 0.0) for h in range(4)], axis=0)

    def quad_body(q, carry):
        xr_f = bd(rt_scr[q])
        xa = _hl(bd(at_scr[q]))
        xr = _hl(xr_f)
        yk = _hl(bd(kh_scr[q]))
        yb = _hl(bd(bh_scr[q]))
        vb = _hl(bd(v_scr[q]))
        kp = _hl(bd(kp_scr[q]))
        bp = _hl(bd(bp_scr[q]))
        a_ak = _hl(jnp.where(tril_s, _dot3(xa, yk, _NT), 0.0))
        a_ab = jnp.where(tril_s, _dot3(xa, yb, _NT), 0.0)
        a_rk = _hl(jnp.where(tril_i, _dot3(xr, yk, _NT), 0.0))
        a_rb = _hl(jnp.where(tril_i, _dot3(xr, yb, _NT), 0.0))
        tinv = eye_r + a_ab
        p = a_ab
        n = 2
        while n < chunk:
            pb = p.astype(BF16)
            p = lax.dot_general(pb, pb, _NN, preferred_element_type=F32)
            tinv = tinv + _dot(tinv, p)
            n *= 2
        tinv = _hl(tinv)
        av = _hl(_dot3(a_ak, vb, _NN))
        ap = _hl(_dot3(tinv, xa, _NN))
        u0 = _hl(_dot3(tinv, av, _NN))
        rp = _hl(xr_f + _dot3(a_rb, ap, _NN))
        o0 = _dot3(a_rk, vb, _NN) + _dot3(a_rb, u0, _NN)
        dg = jnp.where(qi == qj, dg_scr[q], 0.0)
        mm = _hl(dg + _dot3(ap, bp, _TN))
        sadd = _dot3(vb, kp, _TN) + _dot3(u0, bp, _TN)
        s = _hl(s_scr[q])
        o = _dot3(rp, s, _NT) + o0
        s_new = _dot3(s, mm, _NN) + sadd
        s_scr[q] = s_new
        oq_scr[q] = o[0:chunk] + o[chunk:2 * chunk] + o[2 * chunk:3 * chunk] + o[3 * chunk:4 * chunk]
        return carry

    lax.fori_loop(0, nquad, quad_body, 0, unroll=4)
    for q in range(nquad):
        o_ref[:, q * QUAD:(q + 1) * QUAD] = oq_scr[q]

    @pl.when(c == pl.num_programs(1) - 1)
    def _():
        for q in range(nquad):
            sq = s_scr[q]
            sout_ref[:, q * QUAD:(q + 1) * QUAD] = (sq[0:HEAD] + sq[HEAD:2 * HEAD]
                                                    + sq[2 * HEAD:3 * HEAD] + sq[3 * HEAD:4 * HEAD])


def _wkv(r, lw, k, v, a, b, s0, chunk):
    bsz, t, d = lw.shape
    nquad = d // QUAD
    seq = pl.BlockSpec((None, chunk, d), lambda bi, ci: (bi, ci, 0))
    r_spec = pl.BlockSpec((None, None, chunk, d), lambda bi, ci: (0, bi, ci, 0)) if r.ndim == 4 else seq
    st = pl.BlockSpec((None, HEAD, d), lambda bi, ci: (bi, 0, 0))
    cd = pltpu.VMEM((nquad, chunk, QUAD), F32)
    return pl.pallas_call(
        functools.partial(_wkv_kernel, chunk=chunk, nquad=nquad),
        grid=(bsz, t // chunk),
        in_specs=[r_spec] + [seq] * 5 + [st],
        out_specs=[seq, st],
        out_shape=[jax.ShapeDtypeStruct((bsz, t, d), F32), jax.ShapeDtypeStruct((bsz, HEAD, d), F32)],
        scratch_shapes=[pltpu.VMEM((nquad, QUAD, QUAD), F32), cd, cd, cd, cd, cd, cd, cd, cd,
                        pltpu.VMEM((nquad, 1, QUAD), F32)],
        compiler_params=_cparams(2),
        name="wkv_scan",
    )(r, lw, k, v, a, b, s0)


def _post_kernel(o_ref, rkv_ref, k_ref, v_ref, g_ref, lw_ref, lb_ref, rk_ref, e_ref, et_ref, y_ref):
    o = o_ref[...]
    e = e_ref[...]
    et = et_ref[...]
    inv_n = 1.0 / HEAD
    mu = _dot_exact_rhs01(_dot_exact_rhs01(o, e) * inv_n, et)
    dlt = o - mu
    var = _dot_exact_rhs01(dlt * dlt, e) * inv_n
    rstd = _dot_exact_rhs01(lax.rsqrt(var + LNX_EPS), et)
    on = dlt * rstd * lw_ref[...] + lb_ref[...]
    rks = _dot_exact_rhs01(rkv_ref[...] * k_ref[...] * rk_ref[...], e)
    bonus = _dot_exact_rhs01(rks, et) * v_ref[...]
    y_ref[...] = ((on + bonus) * g_ref[...]).astype(BF16)


def _post(o, rkv, k, v, g, lnx_w, lnx_b, r_k, tm=128):
    m, d = o.shape
    tm = _tile(m, tm)
    e, et = _head_selectors(d)
    row = pl.BlockSpec((tm, d), lambda i: (i, 0))
    full = lambda a: pl.BlockSpec(a.shape, lambda i: (0,) * a.ndim)
    return pl.pallas_call(
        _post_kernel,
        grid=(m // tm,),
        in_specs=[row, pl.BlockSpec((None, tm, d), lambda i: (0, i, 0)), row, row, row,
                  full(lnx_w), full(lnx_b), full(r_k), full(e), full(et)],
        out_specs=row,
        out_shape=jax.ShapeDtypeStruct((m, d), BF16),
        compiler_params=_cparams(1),
        name="rwkv_post",
    )(o, rkv, k, v, g, lnx_w, lnx_b, r_k, e, et)


def _t5_bucket(dist):
    max_exact = N_BUCKETS // 2
    n = jnp.maximum(dist, 0)
    nf = jnp.maximum(n, max_exact).astype(F32)
    large = max_exact + (jnp.log(nf / max_exact) / math.log(MAX_DISTANCE / max_exact)
                         * (N_BUCKETS - max_exact)).astype(jnp.int32)
    return jnp.where(n < max_exact, n, jnp.minimum(large, N_BUCKETS - 1))


def _masked_bias(dist, table):
    bias = jnp.transpose(table[_t5_bucket(dist)], (2, 0, 1)).astype(F32)
    valid = (dist >= 0) & (dist < WINDOW)
    return jnp.where(valid[None], bias, NEG)


def _swa_prompt_kernel(sink_ref, q_ref, kp_ref, kc_ref, vp_ref, vc_ref, bias_ref, o_ref, *, nheads, scale):
    first = pl.program_id(1) == 0
    col = lax.broadcasted_iota(jnp.int32, (WINDOW, 2 * WINDOW), 1)
    hide_prev = first & (col < WINDOW)
    for g in range(nheads // GROUP):
        ks = slice(g * HEAD, (g + 1) * HEAD)
        kw = jnp.concatenate([kp_ref[:, ks], kc_ref[:, ks]], axis=0).astype(BF16)
        vw = jnp.concatenate([vp_ref[:, ks], vc_ref[:, ks]], axis=0).astype(BF16)
        for hh in range(GROUP):
            h = g * GROUP + hh
            qh = q_ref[:, h * HEAD:(h + 1) * HEAD]
            s = _dot_nt(qh, kw) * scale + bias_ref[h]
            s = jnp.where(hide_prev, NEG, s)
            sk = sink_ref[h]
            m = jnp.maximum(jnp.max(s, axis=-1, keepdims=True), sk)
            p = jnp.exp(s - m)
            p = p / (jnp.sum(p, axis=-1, keepdims=True) + jnp.exp(sk - m))
            o_ref[:, h * HEAD:(h + 1) * HEAD] = _dot(p, vw).astype(o_ref.dtype)


def _swa_prompt(qkv, sinks, bias, nheads):
    bsz, t, _ = qkv.shape
    d = nheads * HEAD
    kvw = (nheads // GROUP) * HEAD
    nb = t // WINDOW
    kblk = d // kvw
    qspec = pl.BlockSpec((None, WINDOW, d), lambda b, n, s: (b, n, 0))
    kprev = pl.BlockSpec((None, WINDOW, kvw), lambda b, n, s: (b, jnp.maximum(n - 1, 0), kblk))
    kcur = pl.BlockSpec((None, WINDOW, kvw), lambda b, n, s: (b, n, kblk))
    vprev = pl.BlockSpec((None, WINDOW, kvw), lambda b, n, s: (b, jnp.maximum(n - 1, 0), kblk + 1))
    vcur = pl.BlockSpec((None, WINDOW, kvw), lambda b, n, s: (b, n, kblk + 1))
    bspec = pl.BlockSpec(bias.shape, lambda b, n, s: (0, 0, 0))
    return pl.pallas_call(
        functools.partial(_swa_prompt_kernel, nheads=nheads, scale=HEAD ** -0.5),
        grid_spec=pltpu.PrefetchScalarGridSpec(
            num_scalar_prefetch=1,
            grid=(bsz, nb),
            in_specs=[qspec, kprev, kcur, vprev, vcur, bspec],
            out_specs=pl.BlockSpec((None, WINDOW, d), lambda b, n, s: (b, n, 0)),
        ),
        out_shape=jax.ShapeDtypeStruct((bsz, t, d), BF16),
        compiler_params=_cparams(2),
        name="swa_prompt",
    )(sinks, qkv, qkv, qkv, qkv, qkv, bias)


def _swa_sample_kernel(sink_ref, q_ref, kn_ref, vn_ref, ck_ref, cv_ref, bc_ref, bn_ref, o_ref, *, nheads, scale):
    for g in range(nheads // GROUP):
        ks = slice(g * HEAD, (g + 1) * HEAD)
        kc = ck_ref[:, ks].astype(BF16)
        vc = cv_ref[:, ks].astype(BF16)
        kn = kn_ref[:, ks].astype(BF16)
        vn = vn_ref[:, ks].astype(BF16)
        for hh in range(GROUP):
            h = g * GROUP + hh
            qh = q_ref[:, h * HEAD:(h + 1) * HEAD]
            sc = _dot_nt(qh, kc) * scale + bc_ref[h]
            sn = _dot_nt(qh, kn) * scale + bn_ref[h]
            sk = sink_ref[h]
            m = jnp.maximum(jnp.maximum(jnp.max(sc, axis=-1, keepdims=True),
                                        jnp.max(sn, axis=-1, keepdims=True)), sk)
            pc = jnp.exp(sc - m)
            pn = jnp.exp(sn - m)
            den = jnp.sum(pc, axis=-1, keepdims=True) + jnp.sum(pn, axis=-1, keepdims=True) + jnp.exp(sk - m)
            o = _dot(pc / den, vc) + _dot(pn / den, vn)
            o_ref[:, h * HEAD:(h + 1) * HEAD] = o.astype(o_ref.dtype)


def _swa_sample(qkv, ck, cv, lyr, sinks, bias_c, bias_n, nheads):
    bsz, l, _ = qkv.shape
    d = nheads * HEAD
    kvw = (nheads // GROUP) * HEAD
    kblk = d // kvw
    nrows = ck.shape[2]
    full = lambda a: pl.BlockSpec(a.shape, lambda b, s: (0,) * a.ndim)
    return pl.pallas_call(
        functools.partial(_swa_sample_kernel, nheads=nheads, scale=HEAD ** -0.5),
        grid_spec=pltpu.PrefetchScalarGridSpec(
            num_scalar_prefetch=1,
            grid=(bsz,),
            in_specs=[pl.BlockSpec((None, l, d), lambda b, s: (b, 0, 0)),
                      pl.BlockSpec((None, l, kvw), lambda b, s: (b, 0, kblk)),
                      pl.BlockSpec((None, l, kvw), lambda b, s: (b, 0, kblk + 1)),
                      pl.BlockSpec((None, None, nrows, kvw), lambda b, s: (lyr, b, 0, 0)),
                      pl.BlockSpec((None, None, nrows, kvw), lambda b, s: (lyr, b, 0, 0)),
                      full(bias_c), full(bias_n)],
            out_specs=pl.BlockSpec((None, l, d), lambda b, s: (b, 0, 0)),
        ),
        out_shape=jax.ShapeDtypeStruct((bsz, l, d), BF16),
        compiler_params=_cparams(1),
        name="swa_sample",
    )(sinks, qkv, qkv, qkv, ck, cv, bias_c, bias_n)


def _router_kernel(h_ref, w_ref, b_ref, idx_ref, gate_ref, rank_ref, cnt_ref, run_scr, *, n_exp):
    i = pl.program_id(0)
    tm = h_ref.shape[0]

    @pl.when(i == 0)
    def _():
        run_scr[...] = jnp.zeros_like(run_scr)

    logits = _dot(h_ref[...], w_ref[...]) + b_ref[...]
    lane = lax.broadcasted_iota(jnp.int32, (tm, n_exp), 1)
    out_lane = lax.broadcasted_iota(jnp.int32, (tm, 128), 1)
    ri = lax.broadcasted_iota(jnp.int32, (tm, tm), 0)
    rj = lax.broadcasted_iota(jnp.int32, (tm, tm), 1)
    work = logits
    vals, sels = [], []
    chosen = jnp.zeros((tm, n_exp), F32)
    idx_out = jnp.zeros((tm, 128), jnp.int32)
    for kk in range(TOP_K):
        mx = jnp.max(work, axis=-1, keepdims=True)
        idx = jnp.min(jnp.where(work == mx, lane, n_exp), axis=-1, keepdims=True)
        sel = lane == idx
        vals.append(mx)
        sels.append(sel)
        chosen = chosen + sel.astype(F32)
        idx_out = jnp.where(out_lane == kk, idx, idx_out)
        work = jnp.where(sel, -jnp.inf, work)
    es = [jnp.exp(v - vals[0]) for v in vals]
    den = es[0] + es[1] + es[2] + es[3]
    before = jnp.dot((ri > rj).astype(BF16), chosen.astype(BF16), preferred_element_type=F32)
    pos = before + run_scr[...]
    gate_out = jnp.zeros((tm, 128), F32)
    rank_out = jnp.zeros((tm, 128), jnp.int32)
    for kk in range(TOP_K):
        gate_out = jnp.where(out_lane == kk, es[kk] / den, gate_out)
        rk = jnp.sum(jnp.where(sels[kk], pos, 0.0), axis=-1, keepdims=True).astype(jnp.int32)
        rank_out = jnp.where(out_lane == kk, rk, rank_out)
    run_scr[...] = run_scr[...] + jnp.sum(chosen, axis=0, keepdims=True)
    idx_ref[...] = idx_out
    gate_ref[...] = gate_out
    rank_ref[...] = rank_out
    cnt_ref[...] = run_scr[...]


def _router(hb, w_router, b_router, lyr, tm=256):
    t, d = hb.shape
    n_exp = w_router.shape[-1]
    tm = _tile(t, tm)
    wide = pl.BlockSpec((tm, 128), lambda i: (i, 0))
    return pl.pallas_call(
        functools.partial(_router_kernel, n_exp=n_exp),
        grid=(t // tm,),
        in_specs=[pl.BlockSpec((tm, d), lambda i: (i, 0)),
                  pl.BlockSpec((None, d, n_exp), lambda i: (lyr, 0, 0)),
                  pl.BlockSpec((None, 1, n_exp), lambda i: (lyr, 0, 0))],
        out_specs=[wide, wide, wide, pl.BlockSpec((1, n_exp), lambda i: (0, 0))],
        out_shape=[jax.ShapeDtypeStruct((t, 128), jnp.int32), jax.ShapeDtypeStruct((t, 128), F32),
                   jax.ShapeDtypeStruct((t, 128), jnp.int32), jax.ShapeDtypeStruct((1, n_exp), F32)],
        scratch_shapes=[pltpu.VMEM((1, n_exp), F32)],
        compiler_params=_cparams(1),
        name="moe_router",
    )(hb, w_router, b_router)


def _weight_copies(w_refs, stage, sems, lyr, e, col, width, slot):
    return [pltpu.make_async_copy(w.at[lyr, e, :, pl.ds(col, width)], stage.at[slot, m], sems.at[slot, m])
            for m, w in enumerate(w_refs)]


def _stream_expert_weights(te_ref, rid_ref, re_ref, nr_ref, w_refs, stage, sems, wbf_refs, lyr, width):
    j = pl.program_id(0)
    i = pl.program_id(1)
    e = te_ref[i]
    first = (i == 0) | (e != te_ref[jnp.maximum(i - 1, 0)])
    n_runs = nr_ref[0]
    run = rid_ref[i]
    g = j * n_runs + run
    slot = g % 2

    @pl.when(first)
    def _():
        col = pl.multiple_of(j * width, width)

        @pl.when(g == 0)
        def _():
            for c in _weight_copies(w_refs, stage, sems, lyr, e, col, width, slot):
                c.start()

        for c in _weight_copies(w_refs, stage, sems, lyr, e, col, width, slot):
            c.wait()
        wrap = run + 1 >= n_runs
        run_n = jnp.where(wrap, 0, run + 1)
        j_n = jnp.where(wrap, j + 1, j)

        @pl.when(j_n < pl.num_programs(0))
        def _():
            col_n = pl.multiple_of(j_n * width, width)
            for c in _weight_copies(w_refs, stage, sems, lyr, re_ref[run_n], col_n, width, 1 - slot):
                c.start()

        for m, wbf in enumerate(wbf_refs):
            wbf[...] = stage[slot, m].astype(BF16)


def _expert_up_kernel(te_ref, nu_ref, rid_ref, re_ref, nr_ref, x_ref, wg_hbm, wu_hbm, bg_ref, bu_ref, o_ref,
                      stage, sems, wg_bf, wu_bf, *, lyr, tf):
    i = pl.program_id(1)
    _stream_expert_weights(te_ref, rid_ref, re_ref, nr_ref, (wg_hbm, wu_hbm), stage, sems, (wg_bf, wu_bf), lyr, tf)

    @pl.when(i < nu_ref[0])
    def _():
        x = x_ref[...].astype(BF16)
        glu = jnp.minimum(jnp.dot(x, wg_bf[...], preferred_element_type=F32) + bg_ref[...], SWIGLU_LIMIT)
        lin = jnp.clip(jnp.dot(x, wu_bf[...], preferred_element_type=F32) + bu_ref[...],
                       -SWIGLU_LIMIT, SWIGLU_LIMIT)
        o_ref[...] = (glu * jax.nn.sigmoid(SWIGLU_ALPHA * glu) * (lin + 1.0)).astype(o_ref.dtype)

    @pl.when(i >= nu_ref[0])
    def _():
        o_ref[...] = jnp.zeros_like(o_ref)


def _expert_up(sched, xs, w_gate, w_up, b_gate, b_up, lyr, tf=1024):
    n_slots, d = xs.shape
    f = w_gate.shape[-1]
    tf = _tile(f, tf)
    n_tiles = n_slots // EXPERT_TILE
    bspec = pl.BlockSpec((None, None, 1, tf), lambda j, i, te, *_: (lyr, te[i], 0, j))
    hbm = pl.BlockSpec(memory_space=pl.ANY)
    return pl.pallas_call(
        functools.partial(_expert_up_kernel, lyr=lyr, tf=tf),
        grid_spec=pltpu.PrefetchScalarGridSpec(
            num_scalar_prefetch=5,
            grid=(f // tf, n_tiles),
            in_specs=[pl.BlockSpec((EXPERT_TILE, d), lambda j, i, *_: (i, 0)), hbm, hbm, bspec, bspec],
            out_specs=pl.BlockSpec((EXPERT_TILE, tf), lambda j, i, *_: (i, j)),
            scratch_shapes=[pltpu.VMEM((2, 2, d, tf), F32), pltpu.SemaphoreType.DMA((2, 2)),
                            pltpu.VMEM((d, tf), BF16), pltpu.VMEM((d, tf), BF16)],
        ),
        out_shape=jax.ShapeDtypeStruct((n_slots, f), BF16),
        compiler_params=_cparams(2),
        name="moe_expert_up",
    )(*sched, xs, w_gate, w_up, b_gate, b_up)


def _expert_down_kernel(te_ref, nu_ref, rid_ref, re_ref, nr_ref, h_ref, wd_hbm, bd_ref, o_ref,
                        stage, sems, wd_bf, *, lyr, tn):
    i = pl.program_id(1)
    _stream_expert_weights(te_ref, rid_ref, re_ref, nr_ref, (wd_hbm,), stage, sems, (wd_bf,), lyr, tn)

    @pl.when(i < nu_ref[0])
    def _():
        o_ref[...] = jnp.dot(h_ref[...], wd_bf[...], preferred_element_type=F32) + bd_ref[...]

    @pl.when(i >= nu_ref[0])
    def _():
        o_ref[...] = jnp.zeros_like(o_ref)


def _expert_down(sched, hs, w_down, b_down, lyr, tn=2048):
    n_slots, f = hs.shape
    d = w_down.shape[-1]
    tn = _tile(d, tn)
    n_tiles = n_slots // EXPERT_TILE
    return pl.pallas_call(
        functools.partial(_expert_down_kernel, lyr=lyr, tn=tn),
        grid_spec=pltpu.PrefetchScalarGridSpec(
            num_scalar_prefetch=5,
            grid=(d // tn, n_tiles),
            in_specs=[pl.BlockSpec((EXPERT_TILE, f), lambda j, i, *_: (i, 0)),
                      pl.BlockSpec(memory_space=pl.ANY),
                      pl.BlockSpec((None, None, 1, tn), lambda j, i, te, *_: (lyr, te[i], 0, j))],
            out_specs=pl.BlockSpec((EXPERT_TILE, tn), lambda j, i, *_: (i, j)),
            scratch_shapes=[pltpu.VMEM((2, 1, f, tn), F32), pltpu.SemaphoreType.DMA((2, 1)),
                            pltpu.VMEM((f, tn), BF16)],
        ),
        out_shape=jax.ShapeDtypeStruct((n_slots, d), F32),
        compiler_params=_cparams(2),
        name="moe_expert_down",
    )(*sched, hs, w_down, b_down)


def _moe(h, lyr, P):
    t, d = h.shape
    n_exp = P["m_w_router"].shape[-1]
    idx, gate, rank, cnt = _router(h, P["m_w_router"], P["m_b_router"].reshape(-1, 1, n_exp), lyr)
    counts = cnt[0].astype(jnp.int32)
    padded = (counts + EXPERT_TILE - 1) // EXPERT_TILE * EXPERT_TILE
    pend = jnp.cumsum(padded)
    slot = (pend - padded)[idx[:, :TOP_K]] + rank[:, :TOP_K]
    n_tiles = -(-(t * TOP_K) // EXPERT_TILE) + n_exp
    n_slots = n_tiles * EXPERT_TILE
    tok = jnp.broadcast_to(jnp.arange(t, dtype=jnp.int32)[:, None], (t, TOP_K))
    slot_tok = jnp.full((n_slots,), t, jnp.int32).at[slot.reshape(-1)].set(tok.reshape(-1))
    tile_start = jnp.arange(n_tiles, dtype=jnp.int32) * EXPERT_TILE
    tile_e = jnp.minimum(jnp.sum((pend[None, :] <= tile_start[:, None]).astype(jnp.int32), axis=1), n_exp - 1)
    n_used = (pend[-1:] // EXPERT_TILE).astype(jnp.int32)
    change = jnp.concatenate([jnp.ones((1,), jnp.int32), (tile_e[1:] != tile_e[:-1]).astype(jnp.int32)])
    run_id = jnp.cumsum(change) - 1
    run_e = jnp.zeros((n_tiles,), jnp.int32).at[run_id].set(tile_e)
    sched = (tile_e, n_used, run_id, run_e, run_id[-1:] + 1)
    xs = jnp.concatenate([h, jnp.zeros((1, d), h.dtype)], axis=0)[slot_tok]
    f = P["m_w_gate"].shape[-1]
    hs = _expert_up(sched, xs, P["m_w_gate"], P["m_w_up"],
                    P["m_b_gate"].reshape(-1, n_exp, 1, f), P["m_b_up"].reshape(-1, n_exp, 1, f), lyr)
    ys = _expert_down(sched, hs, P["m_w_down"], P["m_b_down"].reshape(-1, n_exp, 1, d), lyr)
    return [ys[slot[:, kk]] for kk in range(TOP_K)], gate


def _combine_kernel(x_ref, y0_ref, y1_ref, y2_ref, y3_ref, w_ref, g_ref, o_ref):
    w = w_ref[...]
    y = (y0_ref[...] * w[:, 0:1] + y1_ref[...] * w[:, 1:2]) + (y2_ref[...] * w[:, 2:3] + y3_ref[...] * w[:, 3:4])
    o_ref[...] = x_ref[...] + g_ref[...] * y


def _combine(x, parts, weights, row0, gate, rows_per_seq, tm=512):
    m, d = x.shape
    tm = _tile(rows_per_seq if gate.ndim == 3 else m, tm)
    assert row0 % tm == 0
    off = row0 // tm
    row = pl.BlockSpec((tm, d), lambda i: (i, 0))
    shifted = pl.BlockSpec((tm, d), lambda i: (i + off, 0))
    return pl.pallas_call(
        _combine_kernel,
        grid=(m // tm,),
        in_specs=[row] + [shifted] * TOP_K + [pl.BlockSpec((tm, 128), lambda i: (i + off, 0)),
                                              _vec_spec(gate, tm, d, rows_per_seq)],
        out_specs=row,
        out_shape=jax.ShapeDtypeStruct((m, d), F32),
        compiler_params=_cparams(1),
        name="moe_combine",
    )(x, *parts, weights, gate)


class _Group:
    def __init__(self, x, per_row_mod):
        self.n, self.t, self.d = x.shape
        self.x = x.reshape(self.n * self.t, self.d)
        self.per_row = per_row_mod

    def vec(self, v):
        if self.per_row:
            return jnp.repeat(v, self.t, axis=0)
        return v[:, None, :]


def _rwkv_layer(grp, h, shift0, s0, v_first, j, P, chunk):
    m, d = h.shape
    h3 = h.reshape(grp.n, grp.t, d)
    hp = jnp.concatenate([shift0[:, None, :], h3[:, :-1]], axis=1).reshape(m, d)
    xm = _mix(h, hp, P["a_mix"][j])
    rkv = _mm(xm, P["a_w_rkv"], wlead=(j, "g"), groups=3)
    tw = _mm(xm, P["a_w1"], wlead=(j,), xoff=3, act="tanh", out_dtype=BF16)[0]
    ta = _mm(xm, P["a_a1"], wlead=(j,), xoff=4, out_dtype=BF16)[0]
    tg = _mm(xm, P["a_g1"], wlead=(j,), xoff=5, act="sigmoid", out_dtype=BF16)[0]
    has_vres = j > 0
    if has_vres:
        tv = _mm(xm, P["a_v1"], wlead=(j - 1,), xoff=2, out_dtype=BF16)[0]
        v2, v0, vf = P["a_v2"][j - 1], P["a_v0"][j - 1][None], v_first
    else:
        tv = jnp.zeros((m, 8), BF16)
        v2, v0, vf = jnp.zeros((8, d), F32), jnp.zeros((1, d), F32), jnp.zeros((m, 8), F32)
    lw, k, v, a, b, g = _prep(rkv, vf, tw, ta, tv, tg, P["a_w2"][j], P["a_a2"][j], v2, P["a_g2"][j],
                              P["a_w0"][j][None], P["a_a0"][j][None], v0,
                              P["a_k_k"][j][None], P["a_k_a"][j][None], has_vres)
    if not has_vres:
        v_first = rkv
    t_pad = -(-grp.t // chunk) * chunk

    def seq(z):
        z = z.reshape(grp.n, grp.t, d)
        if t_pad != grp.t:
            z = jnp.pad(z, ((0, 0), (0, t_pad - grp.t), (0, 0)))
        return z

    r_in = rkv.reshape(3, grp.n, grp.t, d) if t_pad == grp.t else seq(rkv[0])
    o, s_out = _wkv(r_in, seq(lw), seq(k), seq(v), seq(a), seq(b), s0, chunk)
    o = o[:, :grp.t].reshape(m, d)
    y_in = _post(o, rkv, k, v, g, P["a_lnx_w"][j][None], P["a_lnx_b"][j][None],
                 P["a_r_k"][j].reshape(1, d))
    return y_in, s_out, h3[:, -1], v_first


def _state_to_slab(s):
    bsz, nh, hv, hk = s.shape
    return jnp.transpose(s, (0, 2, 1, 3)).reshape(bsz, hv, nh * hk)


def _slab_to_state(s, nh):
    bsz, hv, _ = s.shape
    return jnp.transpose(s.reshape(bsz, hv, nh, HEAD), (0, 2, 1, 3))


def kernel(x_prompt, x_sample, c_prompt, c_sample, state_wkv, state_shift, cache_k, cache_v, w_ada, b_ada, norm1_g, norm2_g, final_g, rel_bias_table, a_mix, a_w_rkv, a_w0, a_w1, a_w2, a_a0, a_a1, a_a2, a_v0, a_v1, a_v2, a_g1, a_g2, a_k_k, a_k_a, a_r_k, a_lnx_w, a_lnx_b, a_w_o, b_w_qkv, b_b_qkv, b_sinks, b_w_o, b_b_o, m_w_router, m_b_router, m_w_gate, m_b_gate, m_w_up, m_b_up, m_w_down, m_b_down):
    P = dict(a_mix=a_mix, a_w_rkv=a_w_rkv, a_w0=a_w0, a_w1=a_w1, a_w2=a_w2, a_a0=a_a0, a_a1=a_a1,
             a_a2=a_a2, a_v0=a_v0, a_v1=a_v1, a_v2=a_v2, a_g1=a_g1, a_g2=a_g2, a_k_k=a_k_k,
             a_k_a=a_k_a, a_r_k=a_r_k, a_lnx_w=a_lnx_w, a_lnx_b=a_lnx_b,
             m_w_router=m_w_router, m_b_router=m_b_router, m_w_gate=m_w_gate, m_b_gate=m_b_gate,
             m_w_up=m_w_up, m_b_up=m_b_up, m_w_down=m_w_down, m_b_down=m_b_down)
    depth = w_ada.shape[0]
    d = x_prompt.shape[-1]
    nheads = d // HEAD
    n_p, n_s = x_prompt.shape[0], x_sample.shape[0]
    n_rows = cache_k.shape[2]
    kvw = (nheads // GROUP) * HEAD

    gp = _Group(x_prompt, per_row_mod=False)
    gs = _Group(x_sample, per_row_mod=True)
    groups = (gp, gs)

    n_c = n_p + n_s
    c_all = jnp.concatenate([c_prompt, c_sample], axis=0)
    c_all = jnp.pad(c_all, ((0, -n_c % 8), (0, 0)))[None]
    mod = _mm(c_all, w_ada, wlead=("g",), bias=b_ada[:, None, :], pre_act="silu", groups=depth, x_shared=True,
              tn=1024)

    qi = jnp.arange(WINDOW)[:, None]
    kj = jnp.arange(2 * WINDOW)[None, :]
    bias_p = _masked_bias(qi + WINDOW - kj, rel_bias_table)
    l_s = gs.t
    dist_s = (n_rows + jnp.arange(l_s)[:, None]) - jnp.arange(n_rows + l_s)[None, :]
    bias_s = _masked_bias(dist_s, rel_bias_table)
    bias_sc, bias_sn = bias_s[:, :, :n_rows], bias_s[:, :, n_rows:]

    s0 = {0: jnp.zeros((state_wkv.shape[0], n_p, HEAD, d), F32),
          1: jax.vmap(_state_to_slab)(state_wkv)}
    shift0 = {0: jnp.zeros((state_shift.shape[0], n_p, d), F32), 1: state_shift}
    ck = cache_k.reshape(cache_k.shape[0], n_s, n_rows, kvw)
    cv = cache_v.reshape(cache_v.shape[0], n_s, n_rows, kvw)
    chunks = {0: min(64, gp.t), 1: 16}

    v_first = {0: None, 1: None}
    wkv_new = {0: [], 1: []}
    shift_new = {0: [], 1: []}
    k_new = {0: [], 1: []}
    v_new = {0: [], 1: []}

    for i in range(depth):
        j = i // 2
        mods = []
        for gi, grp in enumerate(groups):
            lo = 0 if gi == 0 else n_p
            mg = mod[i, lo:lo + grp.n]
            mods.append([grp.vec(z) for z in jnp.split(mg, 6, axis=-1)])
        for gi, grp in enumerate(groups):
            sh1, sc1, gt1 = mods[gi][0], mods[gi][1], mods[gi][2]
            if i % 2 == 0:
                (h,) = _norm_mod(grp.x, norm1_g[i][None], sc1, sh1, grp.t, (F32,))
                y_in, s_out, last, v_first[gi] = _rwkv_layer(grp, h, shift0[gi][j], s0[gi][j], v_first[gi], j, P,
                                                             chunks[gi])
                wkv_new[gi].append(_slab_to_state(s_out, nheads))
                shift_new[gi].append(last)
                grp.x = _mm(y_in[None], a_w_o, wlead=(j,), res=grp.x, gate=gt1, rows_per_seq=grp.t)[0]
            else:
                (hb,) = _norm_mod(grp.x, norm1_g[i][None], sc1, sh1, grp.t, (BF16,))
                qkv = _mm(hb[None], b_w_qkv, wlead=(j,), bias=b_b_qkv[j][None, None, :])[0]
                qkv3 = qkv.reshape(grp.n, grp.t, -1)
                if gi == 0:
                    att = _swa_prompt(qkv3, b_sinks[j], bias_p, nheads)
                    k_new[gi].append(qkv3[:, -n_rows:, d:d + kvw])
                    v_new[gi].append(qkv3[:, -n_rows:, d + kvw:])
                else:
                    att = _swa_sample(qkv3, ck, cv, j, b_sinks[j], bias_sc, bias_sn, nheads)
                    k_new[gi].append(jnp.concatenate([ck[j], qkv3[:, :, d:d + kvw]], axis=1)[:, -n_rows:])
                    v_new[gi].append(jnp.concatenate([cv[j], qkv3[:, :, d + kvw:]], axis=1)[:, -n_rows:])
                grp.x = _mm(att.reshape(1, grp.n * grp.t, d), b_w_o, wlead=(j,), bias=b_b_o[j][None, None, :],
                            res=grp.x, gate=gt1, rows_per_seq=grp.t)[0]
        hbs = []
        for gi, grp in enumerate(groups):
            sh2, sc2 = mods[gi][3], mods[gi][4]
            hbs.append(_norm_mod(grp.x, norm2_g[i][None], sc2, sh2, grp.t, (F32,))[0])
        parts, weights = _moe(jnp.concatenate(hbs, axis=0), i, P)
        lo = 0
        for gi, grp in enumerate(groups):
            grp.x = _combine(grp.x, parts, weights, lo, mods[gi][5], grp.t)
            lo += grp.n * grp.t

    outs = []
    for gi, grp in enumerate(groups):
        y = _final_norm(grp.x, final_g[None]).reshape(grp.n, grp.t, d)
        kv_shape = (len(k_new[gi]), grp.n, n_rows, nheads // GROUP, HEAD)
        outs.append((y, jnp.stack(wkv_new[gi]), jnp.stack(shift_new[gi]),
                     jnp.stack(k_new[gi]).reshape(kv_shape), jnp.stack(v_new[gi]).reshape(kv_shape)))
    return (outs[0][0], outs[1][0]) + outs[0][1:] + outs[1][1:]
```
